```python
import math
import jax
import jax.numpy as jnp
from jax import lax
import numpy as np

D_MODEL = 1024
BATCH = 4
SEQ = 8192
DEPTH = 2

CHUNK = 64
Q_BLOCK = 128
MEM_LEN = 256
EPS = 1e-6

MLA_HEADS = 4
MLA_Q_RANK = 256
MLA_KV_RANK = 128
MLA_NOPE = 64
MLA_ROPE = 32
MLA_V = 128
ROPE_THETA = 10000.0
FOX_HEADS = 4
FOX_DIM = 64
CHK_HEADS = 4
CHK_DIM = 64
CHK_LEFT = 8
BAND = (CHK_LEFT + 1) * CHUNK
REL_MAX = 128
REL_SIZE = (CHUNK - 1) + REL_MAX + 1

A_WIDTH = MLA_HEADS * MLA_V
B_WIDTH = FOX_HEADS * FOX_DIM
C_WIDTH = CHK_HEADS * CHK_DIM
MIX_WIDTH = A_WIDTH + B_WIDTH + C_WIDTH

IN_SIZES = (MLA_Q_RANK, MLA_KV_RANK, MLA_ROPE,
            B_WIDTH, B_WIDTH, B_WIDTH, FOX_HEADS,
            C_WIDTH, C_WIDTH, C_WIDTH)
IN_WIDTH = sum(IN_SIZES)
IN_SPLIT_POINTS = tuple(int(v) for v in np.cumsum(IN_SIZES)[:-1])

CROSS_HEADS = 4
CROSS_DIM = 128
CROSS_WIDTH = CROSS_HEADS * CROSS_DIM

FFN_HIDDEN = ((-(-(8 * D_MODEL) // 3) + 255) // 256) * 256

kernel_name = 'hybrid_mla_fox_chunkrel_block'


def _rmsnorm(x, g):
    xf = x.astype(jnp.float32)
    y = xf * lax.rsqrt(jnp.mean(xf * xf, axis=-1, keepdims=True) + EPS)
    return (y * g.astype(jnp.float32)).astype(x.dtype)


def _rope_tables(seq):
    pos = jnp.arange(seq, dtype=jnp.float32)
    inv = ROPE_THETA ** (-jnp.arange(0, MLA_ROPE, 2, dtype=jnp.float32) / MLA_ROPE)
    ang = pos[:, None] * inv[None, :]
    return jnp.cos(ang), jnp.sin(ang)


def _rope(x, cos, sin):
    xf = x.astype(jnp.float32)
    x1, x2 = jnp.split(xf, 2, axis=-1)
    return jnp.concatenate([x1 * cos - x2 * sin, x1 * sin + x2 * cos], axis=-1).astype(x.dtype)


def _split_blocks(a, block):
    b, s = a.shape[:2]
    return jnp.moveaxis(a.reshape((b, s // block, block) + a.shape[2:]), 1, 0)


def _merge_blocks(a):
    nb, b, blk = a.shape[:3]
    return jnp.moveaxis(a, 0, 1).reshape(b, nb * blk, -1)


def _mla(c_q, c_kv, k_rope, q_norm, w_uq, kv_norm, w_ukv, cos, sin):
    b, s, _ = c_q.shape
    q = (_rmsnorm(c_q, q_norm) @ w_uq).reshape(b, s, MLA_HEADS, MLA_NOPE + MLA_ROPE)
    q_nope, q_pe = jnp.split(q, [MLA_NOPE], axis=-1)
    q = jnp.concatenate([q_nope, _rope(q_pe, cos[:, None], sin[:, None])], axis=-1)
    kv = (_rmsnorm(c_kv, kv_norm) @ w_ukv).reshape(b, s, MLA_HEADS, MLA_NOPE + MLA_V)
    k_nope, v = jnp.split(kv, [MLA_NOPE], axis=-1)
    k_pe = _rope(k_rope, cos, sin)
    k = jnp.concatenate(
        [k_nope, jnp.broadcast_to(k_pe[:, :, None, :], (b, s, MLA_HEADS, MLA_ROPE))], axis=-1)
    scale = (MLA_NOPE + MLA_ROPE) ** -0.5
    k_chunk = jnp.arange(s) // CHUNK

    def attend(args):
        qb, idx = args
        q_chunk = (idx * Q_BLOCK + jnp.arange(Q_BLOCK)) // CHUNK
        sc = jnp.einsum('bqhd,bkhd->bhqk', qb, k, preferred_element_type=jnp.float32) * scale
        sc = jnp.where(k_chunk[None, :] <= q_chunk[:, None], sc, -jnp.inf)
        p = jax.nn.softmax(sc, axis=-1).astype(v.dtype)
        return jnp.einsum('bhqk,bkhd->bqhd', p, v)

    out = lax.map(attend, (_split_blocks(q, Q_BLOCK), jnp.arange(s // Q_BLOCK)))
    return _merge_blocks(out)


def _fox(q, k, v, f_logit, f_bias):
    b, s, _ = q.shape
    q = q.reshape(b, s, FOX_HEADS, FOX_DIM)
    k = k.reshape(b, s, FOX_HEADS, FOX_DIM)
    v = v.reshape(b, s, FOX_HEADS, FOX_DIM)
    log_f = jax.nn.log_sigmoid(f_logit.astype(jnp.float32) + f_bias.astype(jnp.float32))
    cum = jnp.cumsum(log_f, axis=1)
    cum_k = jnp.transpose(cum, (0, 2, 1))
    scale = FOX_DIM ** -0.5
    k_pos = jnp.arange(s)

    def attend(args):
        qb, cq, idx = args
        q_pos = idx * Q_BLOCK + jnp.arange(Q_BLOCK)
        sc = jnp.einsum('bqhd,bkhd->bhqk', qb, k, preferred_element_type=jnp.float32) * scale
        sc = sc + jnp.transpose(cq, (0, 2, 1))[..., None] - cum_k[:, :, None, :]
        sc = jnp.where(k_pos[None, :] <= q_pos[:, None], sc, -jnp.inf)
        p = jax.nn.softmax(sc, axis=-1).astype(v.dtype)
        return jnp.einsum('bhqk,bkhd->bqhd', p, v)

    out = lax.map(attend, (_split_blocks(q, Q_BLOCK), _split_blocks(cum, Q_BLOCK),
                           jnp.arange(s // Q_BLOCK)))
    return _merge_blocks(out)


def _chunk_rel(q, k, v, rel_table):
    b, s, _ = q.shape
    q = q.reshape(b, s, CHK_HEADS, CHK_DIM)
    k = k.reshape(b, s, CHK_HEADS, CHK_DIM)
    v = v.reshape(b, s, CHK_HEADS, CHK_DIM)
    pad = CHK_LEFT * CHUNK
    kp = jnp.pad(k, ((0, 0), (pad, 0), (0, 0), (0, 0)))
    vp = jnp.pad(v, ((0, 0), (pad, 0), (0, 0), (0, 0)))
    qi = jnp.arange(CHUNK)
    ki = jnp.arange(BAND)
    rel = qi[:, None] + pad - ki[None, :]
    rel_idx = jnp.clip(rel, -(CHUNK - 1), REL_MAX) + (CHUNK - 1)
    bias = rel_table[:, rel_idx].astype(jnp.float32)
    scale = CHK_DIM ** -0.5

    def attend(args):
        qc, idx = args
        start = idx * CHUNK
        kb = lax.dynamic_slice_in_dim(kp, start, BAND, axis=1)
        vb = lax.dynamic_slice_in_dim(vp, start, BAND, axis=1)
        valid = (start - pad + ki) >= 0
        sc = jnp.einsum('bqhd,bkhd->bhqk', qc, kb, preferred_element_type=jnp.float32) * scale
        sc = jnp.where(valid, sc + bias[None], -jnp.inf)
        p = jax.nn.softmax(sc, axis=-1).astype(vb.dtype)
        return jnp.einsum('bhqk,bkhd->bqhd', p, vb)

    out = lax.map(attend, (_split_blocks(q, CHUNK), jnp.arange(s // CHUNK)))
    return _merge_blocks(out)


def _cross(h, m, w_cq, w_ckv, w_co):
    b, s, _ = h.shape
    n = m.shape[1]
    q = (h @ w_cq).reshape(b, s, CROSS_HEADS, CROSS_DIM)
    k, v = jnp.split(m @ w_ckv, 2, axis=-1)
    k = k.reshape(b, n, CROSS_HEADS, CROSS_DIM)
    v = v.reshape(b, n, CROSS_HEADS, CROSS_DIM)
    sc = jnp.einsum('bshd,bmhd->bhsm', q, k, preferred_element_type=jnp.float32) * (CROSS_DIM ** -0.5)
    p = jax.nn.softmax(sc, axis=-1).astype(v.dtype)
    o = jnp.einsum('bhsm,bmhd->bshd', p, v).reshape(b, s, CROSS_WIDTH)
    return o @ w_co


def setup_inputs(seed: int = 0) -> dict:
    key = jax.random.key(seed)
    ks = jax.random.split(key, 24)

    def nrm(k, shape, fan_in):
        return jax.random.normal(k, shape, jnp.float32) * (fan_in ** -0.5)

    def gain(k, shape):
        return 1.0 + 0.05 * jax.random.normal(k, shape, jnp.float32)

    L = DEPTH
    return {
        'x': jax.random.normal(ks[0], (BATCH, SEQ, D_MODEL), jnp.float32),
        'mem': jax.random.normal(ks[1], (BATCH, MEM_LEN, D_MODEL), jnp.float32),
        'norm_mix': gain(ks[2], (L, D_MODEL)),
        'w_in': nrm(ks[3], (L, D_MODEL, IN_WIDTH), D_MODEL),
        'q_norm': gain(ks[4], (L, MLA_Q_RANK)),
        'w_uq': nrm(ks[5], (L, MLA_Q_RANK, MLA_HEADS * (MLA_NOPE + MLA_ROPE)), MLA_Q_RANK),
        'kv_norm': gain(ks[6], (L, MLA_KV_RANK)),
        'w_ukv': nrm(ks[7], (L, MLA_KV_RANK, MLA_HEADS * (MLA_NOPE + MLA_V)), MLA_KV_RANK),
        'f_bias': jax.random.uniform(ks[8], (L, FOX_HEADS), jnp.float32, minval=1.0, maxval=4.0),
        'rel_bias': 0.3 * jax.random.normal(ks[9], (L, CHK_HEADS, REL_SIZE), jnp.float32),
        'out_norm': gain(ks[10], (L, MIX_WIDTH)),
        'w_o': nrm(ks[11], (L, MIX_WIDTH, D_MODEL), MIX_WIDTH),
        'norm_cross': gain(ks[12], (L, D_MODEL)),
        'norm_mem': gain(ks[13], (L, D_MODEL)),
        'w_cq': nrm(ks[14], (L, D_MODEL, CROSS_WIDTH), D_MODEL),
        'w_ckv': nrm(ks[15], (L, D_MODEL, 2 * CROSS_WIDTH), D_MODEL),
        'w_co': nrm(ks[16], (L, CROSS_WIDTH, D_MODEL), CROSS_WIDTH),
        'norm_ffn': gain(ks[17], (L, D_MODEL)),
        'w_gu': nrm(ks[18], (L, D_MODEL, 2 * FFN_HIDDEN), D_MODEL),
        'w_down': nrm(ks[19], (L, FFN_HIDDEN, D_MODEL), FFN_HIDDEN),
        'final_norm': gain(ks[20], (D_MODEL,)),
    }


def reference(x, mem, norm_mix, w_in, q_norm, w_uq, kv_norm, w_ukv, f_bias, rel_bias,
              out_norm, w_o, norm_cross, norm_mem, w_cq, w_ckv, w_co, norm_ffn, w_gu,
              w_down, final_norm):
    s = x.shape[1]
    cos, sin = _rope_tables(s)
    for l in range(DEPTH):
        h = _rmsnorm(x, norm_mix[l])
        proj = h @ w_in[l]
        (c_q, c_kv, k_rope, fq, fk, fv, f_logit, cq, ck, cv) = jnp.split(
            proj, IN_SPLIT_POINTS, axis=-1)
        ya = _mla(c_q, c_kv, k_rope, q_norm[l], w_uq[l], kv_norm[l], w_ukv[l], cos, sin)
        yb = _fox(fq, fk, fv, f_logit, f_bias[l])
        yc = _chunk_rel(cq, ck, cv, rel_bias[l])
        ga, gb, gc = jnp.split(out_norm[l], [A_WIDTH, A_WIDTH + B_WIDTH])
        y = jnp.concatenate([_rmsnorm(ya, ga), _rmsnorm(yb, gb), _rmsnorm(yc, gc)], axis=-1)
        x = x + y @ w_o[l]
        x = x + _cross(_rmsnorm(x, norm_cross[l]), _rmsnorm(mem, norm_mem[l]),
                       w_cq[l], w_ckv[l], w_co[l])
        h = _rmsnorm(x, norm_ffn[l])
        gate, up = jnp.split(h @ w_gu[l], 2, axis=-1)
        x = x + (jax.nn.silu(gate) * up) @ w_down[l]
    return _rmsnorm(x, final_norm)
```

```python
import functools
import math

import jax
import jax.numpy as jnp
from jax import lax
from jax.experimental import pallas as pl
from jax.experimental.pallas import tpu as pltpu

D_MODEL = 1024
CHUNK = 64
MEM_LEN = 256
EPS = 1e-6
MLA_HEADS = 4
MLA_Q_RANK = 256
MLA_KV_RANK = 128
MLA_NOPE = 64
MLA_ROPE = 32
MLA_V = 128
ROPE_THETA = 10000.0
FOX_HEADS = 4
FOX_DIM = 64
CHK_HEADS = 4
CHK_DIM = 64
CHK_LEFT = 8
BAND = (CHK_LEFT + 1) * CHUNK
REL_MAX = 128
REL_SIZE = (CHUNK - 1) + REL_MAX + 1
A_WIDTH = MLA_HEADS * MLA_V
B_WIDTH = FOX_HEADS * FOX_DIM
C_WIDTH = CHK_HEADS * CHK_DIM
CROSS_HEADS = 4
CROSS_DIM = 128
CROSS_WIDTH = CROSS_HEADS * CROSS_DIM
FFN_HIDDEN = 2816

LANES = 128
LOG2E = 1.4426950408889634
NEG = -1e30
VMEM_LIMIT = 56 * 1024 * 1024

TOK_TILE = 512
ATT_TILE = 512
CHK_TILE = 256
CHK_WIN = CHK_TILE + CHK_LEFT * CHUNK
FFN_BLOCK = 256
IN_COLS = 2048

BF16 = jnp.bfloat16
F32 = jnp.float32
NT_DIMS = (((1,), (1,)), ((), ()))


def _params(*semantics):
    return pltpu.CompilerParams(dimension_semantics=semantics, vmem_limit_bytes=VMEM_LIMIT)


def _rms(xf, gain, width):
    return xf * lax.rsqrt(jnp.sum(xf * xf, axis=-1, keepdims=True) * (1.0 / width) + EPS) * gain


def _lane_iota(shape):
    return lax.broadcasted_iota(jnp.int32, shape, len(shape) - 1)


def _const_spec(shape):
    zeros = (0,) * len(shape)
    return pl.BlockSpec(shape, lambda *_: zeros)


def _in_proj_kernel(x_ref, g_ref, w_ref, qn_ref, wuq_ref, kvn_ref, wukv_ref, tc_ref, ts1_ref,
                    ts2_ref, fb_ref, tri_ref,
                    mq_ref, mk_ref, mv_ref, fq_ref, fk_ref, fv_ref, cq_ref, ck_ref, cv_ref,
                    cum_ref, carry_ref):
    i = pl.program_id(1)
    tm = x_ref.shape[1]
    h = _rms(x_ref[0], g_ref[...], D_MODEL).astype(BF16)

    def proj(lo, hi):
        return jnp.dot(h, w_ref[:, lo:hi], preferred_element_type=F32)

    lane = _lane_iota((tm, LANES))
    tcos, ts1, ts2 = tc_ref[...], ts1_ref[...], ts2_ref[...]

    def rope(y, keep_low):
        base = tcos + jnp.where(lane < MLA_NOPE, 1.0, 0.0) if keep_low else tcos
        return (y * base + pltpu.roll(y, LANES - MLA_ROPE // 2, 1) * ts1
                + pltpu.roll(y, MLA_ROPE // 2, 1) * ts2)

    c_q = proj(0, 256)
    q_up = jnp.dot(_rms(c_q, qn_ref[...], MLA_Q_RANK).astype(BF16), wuq_ref[...],
                   preferred_element_type=F32)
    misc = proj(384, 512)
    k_pe = rope(misc, False)
    c_kv = proj(256, 384)
    kv_up = jnp.dot(_rms(c_kv, kvn_ref[...], MLA_KV_RANK).astype(BF16), wukv_ref[...],
                    preferred_element_type=F32)
    q_scale = (MLA_NOPE + MLA_ROPE) ** -0.5 * LOG2E
    for hd in range(MLA_HEADS):
        sl = slice(hd * LANES, (hd + 1) * LANES)
        mq_ref[0, hd] = (rope(q_up[:, sl], True) * q_scale).astype(BF16)
        mk_ref[0, hd] = (kv_up[:, sl] + k_pe).astype(BF16)
        mv_ref[0, hd] = kv_up[:, A_WIDTH + hd * LANES:A_WIDTH + (hd + 1) * LANES].astype(BF16)

    def head_block(y, hd):
        blk = y[:, (hd // 2) * LANES:(hd // 2 + 1) * LANES]
        return pltpu.roll(blk, FOX_DIM, 1) if hd % 2 else blk

    fq = proj(512, 768)
    fk = proj(768, 1024)
    fv = proj(1024, 1280)
    f_scale = FOX_DIM ** -0.5 * LOG2E
    ones_col = jnp.where(lane == FOX_DIM, 1.0, 0.0)
    for hd in range(FOX_HEADS):
        fq_ref[0, hd] = jnp.where(lane < FOX_DIM, head_block(fq, hd) * f_scale, 0.0).astype(BF16)
        fk_ref[0, hd] = jnp.where(lane < FOX_DIM, head_block(fk, hd), 0.0).astype(BF16)
        fv_ref[0, hd] = jnp.where(lane < FOX_DIM, head_block(fv, hd), ones_col).astype(BF16)

    z = misc + fb_ref[...]
    log_f = jnp.minimum(z, 0.0) - jnp.log1p(jnp.exp(-jnp.abs(z)))
    log_f = jnp.where(lane < FOX_HEADS, log_f, 0.0)
    p_hi = log_f.astype(BF16)
    rem = log_f - p_hi.astype(F32)
    p_mid = rem.astype(BF16)
    p_lo = (rem - p_mid.astype(F32)).astype(BF16)
    tri = tri_ref[...]
    cum = (jnp.dot(tri, p_hi, preferred_element_type=F32)
           + jnp.dot(tri, p_mid, preferred_element_type=F32)
           + jnp.dot(tri, p_lo, preferred_element_type=F32))

    @pl.when(i == 0)
    def _():
        carry_ref[...] = jnp.zeros_like(carry_ref)

    cum = cum + carry_ref[...]
    carry_ref[...] = cum[tm - 1:tm, :]
    cum_ref[0] = cum * LOG2E

    cq_ref[0] = (proj(1280, 1536) * (CHK_DIM ** -0.5 * LOG2E)).astype(BF16)
    ck_ref[0] = proj(1536, 1792).astype(BF16)
    cv_ref[0] = proj(1792, 2048).astype(BF16)


def _in_proj(x, g, w, qn, wuq, kvn, wukv, tcos, ts1, ts2, fb, tri):
    b, s, _ = x.shape
    tm = TOK_TILE
    head_spec = pl.BlockSpec((1, 4, tm, LANES), lambda bi, i: (bi, 0, i, 0))
    dense_spec = pl.BlockSpec((1, tm, 256), lambda bi, i: (bi, i, 0))
    tab_spec = pl.BlockSpec((tm, LANES), lambda bi, i: (i, 0))
    head_shape = jax.ShapeDtypeStruct((b, 4, s, LANES), BF16)
    dense_shape = jax.ShapeDtypeStruct((b, s, 256), BF16)
    return pl.pallas_call(
        _in_proj_kernel,
        grid=(b, s // tm),
        in_specs=[pl.BlockSpec((1, tm, D_MODEL), lambda bi, i: (bi, i, 0)),
                  _const_spec((1, D_MODEL)), _const_spec((D_MODEL, IN_COLS)),
                  _const_spec((1, MLA_Q_RANK)), _const_spec((MLA_Q_RANK, 512)),
                  _const_spec((1, MLA_KV_RANK)), _const_spec((MLA_KV_RANK, 1024)),
                  tab_spec, tab_spec, tab_spec, _const_spec((1, LANES)), _const_spec((tm, tm))],
        out_specs=[head_spec] * 6 + [dense_spec] * 3
                  + [pl.BlockSpec((1, tm, LANES), lambda bi, i: (bi, i, 0))],
        out_shape=[head_shape] * 6 + [dense_shape] * 3
                  + [jax.ShapeDtypeStruct((b, s, LANES), F32)],
        scratch_shapes=[pltpu.VMEM((1, LANES), F32)],
        compiler_params=_params("arbitrary", "arbitrary"),
        name="in_proj",
    )(x, g, w, qn, wuq, kvn, wukv, tcos, ts1, ts2, fb, tri)


def _sweep_kernel(*refs, decay, chunk_mask):
    if decay:
        q_ref, k_ref, v_ref, cq_ref, ck_ref, o_ref, m_sc, l_sc, acc_sc = refs
    else:
        q_ref, k_ref, v_ref, o_ref, m_sc, l_sc, acc_sc = refs
    i = pl.program_id(2)
    t = ATT_TILE
    q = q_ref[0, 0]
    m_sc[...] = jnp.full_like(m_sc, NEG)
    l_sc[...] = jnp.zeros_like(l_sc)
    acc_sc[...] = jnp.zeros_like(acc_sc)

    def step(j, masked):
        off = pl.multiple_of(j * t, t)
        k = k_ref[0, 0, pl.ds(off, t), :]
        v = v_ref[0, 0, pl.ds(off, t), :]
        s = lax.dot_general(q, k, NT_DIMS, preferred_element_type=F32)
        if decay:
            s = s + cq_ref[0, 0] - ck_ref[0, 0, pl.ds(j, 1), :]
        if masked:
            qi = lax.broadcasted_iota(jnp.int32, (t, t), 0)
            ki = lax.broadcasted_iota(jnp.int32, (t, t), 1)
            if chunk_mask:
                qi, ki = qi // CHUNK, ki // CHUNK
            s = jnp.where(ki <= qi, s, NEG)
        m_old = m_sc[...]
        m_new = jnp.maximum(m_old, jnp.max(s, axis=-1, keepdims=True))
        alpha = jnp.exp2(m_old - m_new)
        p = jnp.exp2(s - m_new)
        if not decay:
            l_sc[...] = alpha * l_sc[...] + jnp.sum(p, axis=-1, keepdims=True)
        acc_sc[...] = alpha * acc_sc[...] + jnp.dot(p.astype(BF16), v,
                                                     preferred_element_type=F32)
        m_sc[...] = m_new

    def body(j, carry):
        step(j, False)
        return carry

    lax.fori_loop(0, i, body, 0)
    step(i, True)
    acc = acc_sc[...]
    denom = acc[:, FOX_DIM:FOX_DIM + 1] if decay else l_sc[...]
    o_ref[0, 0] = acc / denom


def _sweep(q, k, v, cq=None, ck=None, *, chunk_mask):
    b, nh, s, _ = q.shape
    t = ATT_TILE
    decay = cq is not None
    q_spec = pl.BlockSpec((1, 1, t, LANES), lambda bi, hi, i: (bi, hi, i, 0))
    kv_spec = pl.BlockSpec((1, 1, s, LANES), lambda bi, hi, i: (bi, hi, 0, 0))
    in_specs = [q_spec, kv_spec, kv_spec]
    args = [q, k, v]
    if decay:
        in_specs += [pl.BlockSpec((1, 1, t, 1), lambda bi, hi, i: (bi, hi, i, 0)),
                     pl.BlockSpec((1, 1, s // t, t), lambda bi, hi, i: (bi, hi, 0, 0))]
        args += [cq, ck]
    return pl.pallas_call(
        functools.partial(_sweep_kernel, decay=decay, chunk_mask=chunk_mask),
        grid=(b, nh, s // t),
        in_specs=in_specs,
        out_specs=q_spec,
        out_shape=jax.ShapeDtypeStruct((b, nh, s, LANES), F32),
        scratch_shapes=[pltpu.VMEM((t, 1), F32), pltpu.VMEM((t, 1), F32),
                        pltpu.VMEM((t, LANES), F32)],
        compiler_params=_params("arbitrary", "arbitrary", "arbitrary"),
        name="fox_sweep" if decay else "mla_sweep",
    )(*args)


def _rel_bias_kernel(tab_ref, o_ref):
    qi = lax.broadcasted_iota(jnp.int32, (CHUNK, BAND), 0)
    ki = lax.broadcasted_iota(jnp.int32, (CHUNK, BAND), 1)
    idx = jnp.clip(qi + CHK_LEFT * CHUNK - ki, -(CHUNK - 1), REL_MAX) + (CHUNK - 1)
    for hd in range(CHK_HEADS):
        def body(r, acc):
            return jnp.where(idx == r, tab_ref[hd, r], acc)
        o_ref[hd] = lax.fori_loop(0, REL_SIZE, body, jnp.zeros((CHUNK, BAND), F32)) * LOG2E


def _rel_bias(table):
    return pl.pallas_call(
        _rel_bias_kernel,
        in_specs=[pl.BlockSpec(memory_space=pltpu.SMEM)],
        out_specs=pl.BlockSpec(memory_space=pltpu.VMEM),
        out_shape=jax.ShapeDtypeStruct((CHK_HEADS, CHUNK, BAND), F32),
        name="rel_bias",
    )(table)


def _chunk_kernel(q_ref, k_ref, v_ref, bias_ref, o_ref):
    i = pl.program_id(1)
    start = pl.multiple_of(i * CHK_TILE, CHK_TILE)
    q = q_ref[0]
    kw = k_ref[0, pl.ds(start, CHK_WIN), :]
    vw = v_ref[0, pl.ds(start, CHK_WIN), :]
    col = _lane_iota((CHK_TILE, CHK_WIN))
    valid = col >= CHK_LEFT * CHUNK - start
    outs = []
    for hd in range(CHK_HEADS):
        sl = slice(hd * CHK_DIM, (hd + 1) * CHK_DIM)
        s = lax.dot_general(q[:, sl], kw[:, sl], NT_DIMS, preferred_element_type=F32)
        s = jnp.where(valid, s + bias_ref[hd], NEG)
        m = jnp.max(s, axis=-1, keepdims=True)
        p = jnp.exp2(s - m)
        denom = jnp.sum(p, axis=-1, keepdims=True)
        outs.append(jnp.dot(p.astype(BF16), vw[:, sl], preferred_element_type=F32) / denom)
    o_ref[0] = jnp.concatenate(outs, axis=-1)


def _chunk_attn(q, kp, vp, bias):
    b, s, _ = q.shape
    sp = kp.shape[1]
    return pl.pallas_call(
        _chunk_kernel,
        grid=(b, s // CHK_TILE),
        in_specs=[pl.BlockSpec((1, CHK_TILE, C_WIDTH), lambda bi, i: (bi, i, 0)),
                  pl.BlockSpec((1, sp, C_WIDTH), lambda bi, i: (bi, 0, 0)),
                  pl.BlockSpec((1, sp, C_WIDTH), lambda bi, i: (bi, 0, 0)),
                  _const_spec((CHK_HEADS, CHK_TILE, CHK_WIN))],
        out_specs=pl.BlockSpec((1, CHK_TILE, C_WIDTH), lambda bi, i: (bi, i, 0)),
        out_shape=jax.ShapeDtypeStruct((b, s, C_WIDTH), F32),
        compiler_params=_params("arbitrary", "arbitrary"),
        name="chunk_attn",
    )(q, kp, vp, bias)


def _mem_kernel(m_ref, g_ref, w_ref, k_ref, v_ref):
    h = _rms(m_ref[0], g_ref[...], D_MODEL).astype(BF16)
    kv = jnp.dot(h, w_ref[...], preferred_element_type=F32)
    k_ref[0] = kv[:, :CROSS_WIDTH].astype(BF16)
    v_ref[0] = kv[:, CROSS_WIDTH:].astype(BF16)


def _mem_kv(mem, g, w):
    b = mem.shape[0]
    spec = pl.BlockSpec((1, MEM_LEN, CROSS_WIDTH), lambda bi: (bi, 0, 0))
    shape = jax.ShapeDtypeStruct((b, MEM_LEN, CROSS_WIDTH), BF16)
    return pl.pallas_call(
        _mem_kernel,
        grid=(b,),
        in_specs=[pl.BlockSpec((1, MEM_LEN, D_MODEL), lambda bi: (bi, 0, 0)),
                  _const_spec((1, D_MODEL)), _const_spec((D_MODEL, 2 * CROSS_WIDTH))],
        out_specs=[spec, spec],
        out_shape=[shape, shape],
        compiler_params=_params("arbitrary"),
        name="mem_kv",
    )(mem, g, w)


def _mix_cross_kernel(x_ref, ya_ref, yb_ref, yc_ref, ga_ref, gb_ref, gc_ref, woa_ref, wob_ref,
                      woc_ref, gx_ref, wcq_ref, km_ref, vm_ref, wco_ref, o_ref):
    tm = x_ref.shape[1]
    lane = _lane_iota((tm, LANES))
    ya = jnp.concatenate([ya_ref[0, hd] for hd in range(MLA_HEADS)], axis=-1)
    yb = jnp.concatenate([jnp.where(lane < FOX_DIM, yb_ref[0, hd], 0.0)
                          for hd in range(FOX_HEADS)], axis=-1)
    x1 = (x_ref[0]
          + jnp.dot(_rms(ya, ga_ref[...], A_WIDTH).astype(BF16), woa_ref[...],
                    preferred_element_type=F32)
          + jnp.dot(_rms(yb, gb_ref[...], B_WIDTH).astype(BF16), wob_ref[...],
                    preferred_element_type=F32)
          + jnp.dot(_rms(yc_ref[0], gc_ref[...], C_WIDTH).astype(BF16), woc_ref[...],
                    preferred_element_type=F32))
    h = _rms(x1, gx_ref[...], D_MODEL).astype(BF16)
    q = (jnp.dot(h, wcq_ref[...], preferred_element_type=F32)
         * (CROSS_DIM ** -0.5 * LOG2E)).astype(BF16)
    outs = []
    for hd in range(CROSS_HEADS):
        sl = slice(hd * CROSS_DIM, (hd + 1) * CROSS_DIM)
        s = lax.dot_general(q[:, sl], km_ref[0, :, sl], NT_DIMS, preferred_element_type=F32)
        m = jnp.max(s, axis=-1, keepdims=True)
        p = jnp.exp2(s - m)
        denom = jnp.sum(p, axis=-1, keepdims=True)
        outs.append(jnp.dot(p.astype(BF16), vm_ref[0, :, sl], preferred_element_type=F32)
                    / denom)
    o = jnp.concatenate(outs, axis=-1).astype(BF16)
    o_ref[0] = x1 + jnp.dot(o, wco_ref[...], preferred_element_type=F32)


def _mix_cross(x, ya, yb, yc, ga, gb, gc, woa, wob, woc, gx, wcq, km, vm, wco):
    b, s, _ = x.shape
    tm = TOK_TILE
    x_spec = pl.BlockSpec((1, tm, D_MODEL), lambda bi, i: (bi, i, 0))
    head_spec = pl.BlockSpec((1, 4, tm, LANES), lambda bi, i: (bi, 0, i, 0))
    mem_spec = pl.BlockSpec((1, MEM_LEN, CROSS_WIDTH), lambda bi, i: (bi, 0, 0))
    return pl.pallas_call(
        _mix_cross_kernel,
        grid=(b, s // tm),
        in_specs=[x_spec, head_spec, head_spec,
                  pl.BlockSpec((1, tm, C_WIDTH), lambda bi, i: (bi, i, 0)),
                  _const_spec((1, A_WIDTH)), _const_spec((1, 512)), _const_spec((1, C_WIDTH)),
                  _const_spec((A_WIDTH, D_MODEL)), _const_spec((512, D_MODEL)),
                  _const_spec((C_WIDTH, D_MODEL)), _const_spec((1, D_MODEL)),
                  _const_spec((D_MODEL, CROSS_WIDTH)), mem_spec, mem_spec,
                  _const_spec((CROSS_WIDTH, D_MODEL))],
        out_specs=x_spec,
        out_shape=jax.ShapeDtypeStruct(x.shape, F32),
        compiler_params=_params("arbitrary", "arbitrary"),
        name="mix_cross",
    )(x, ya, yb, yc, ga, gb, gc, woa, wob, woc, gx, wcq, km, vm, wco)


def _ffn_kernel(x_ref, g_ref, wg_ref, wu_ref, wd_ref, gf_ref, o_ref, acc_ref, *, final_norm):
    x = x_ref[0]
    h = _rms(x, g_ref[...], D_MODEL).astype(BF16)
    acc_ref[...] = x

    def body(c, carry):
        gate = jnp.dot(h, wg_ref[c], preferred_element_type=F32)
        up = jnp.dot(h, wu_ref[c], preferred_element_type=F32)
        act = (gate * jax.nn.sigmoid(gate) * up).astype(BF16)
        acc_ref[...] += jnp.dot(act, wd_ref[c], preferred_element_type=F32)
        return carry

    lax.fori_loop(0, FFN_HIDDEN // FFN_BLOCK, body, 0)
    y = acc_ref[...]
    o_ref[0] = _rms(y, gf_ref[...], D_MODEL) if final_norm else y


def _ffn(x, g, wg, wu, wd, gf, *, final_norm):
    b, s, _ = x.shape
    tm = TOK_TILE
    nblk = FFN_HIDDEN // FFN_BLOCK
    x_spec = pl.BlockSpec((1, tm, D_MODEL), lambda bi, i: (bi, i, 0))

    def resident(shape):
        return pl.BlockSpec(shape, lambda bi, i: (0, 0, 0), pipeline_mode=pl.Buffered(1))

    return pl.pallas_call(
        functools.partial(_ffn_kernel, final_norm=final_norm),
        grid=(b, s // tm),
        in_specs=[x_spec, _const_spec((1, D_MODEL)),
                  resident((nblk, D_MODEL, FFN_BLOCK)), resident((nblk, D_MODEL, FFN_BLOCK)),
                  resident((nblk, FFN_BLOCK, D_MODEL)), _const_spec((1, D_MODEL))],
        out_specs=x_spec,
        out_shape=jax.ShapeDtypeStruct(x.shape, F32),
        scratch_shapes=[pltpu.VMEM((tm, D_MODEL), F32)],
        compiler_params=_params("arbitrary", "arbitrary"),
        name="ffn",
    )(x, g, wg, wu, wd, gf)


def _pad_cols(a, width):
    return jnp.pad(a, ((0, 0), (0, width - a.shape[1])))


def _layer_weights(w_in, w_uq, w_ukv, w_o, out_norm, w_gu, w_down, f_bias):
    sizes = (256, 128, 32, 256, 256, 256, 4, 256, 256, 256)
    parts, at = [], 0
    for n in sizes:
        parts.append(w_in[:, at:at + n])
        at += n
    c_q, c_kv, k_rope, fq, fk, fv, f_logit, cq, ck, cv = parts
    zeros = lambda n: jnp.zeros((D_MODEL, n), F32)
    misc = jnp.concatenate([f_logit, zeros(MLA_NOPE - FOX_HEADS), k_rope, zeros(32)], axis=1)
    w_in_r = jnp.concatenate([c_q, c_kv, misc, fq, fk, fv, cq, ck, cv], axis=1).astype(BF16)
    wuq = jnp.pad(w_uq.reshape(MLA_Q_RANK, MLA_HEADS, MLA_NOPE + MLA_ROPE),
                  ((0, 0), (0, 0), (0, 32))).reshape(MLA_Q_RANK, 512).astype(BF16)
    wukv = w_ukv.reshape(MLA_KV_RANK, MLA_HEADS, MLA_NOPE + MLA_V)
    wk = jnp.pad(wukv[:, :, :MLA_NOPE], ((0, 0), (0, 0), (0, 64))).reshape(MLA_KV_RANK, 512)
    wv = wukv[:, :, MLA_NOPE:].reshape(MLA_KV_RANK, 512)
    wukv_r = jnp.concatenate([wk, wv], axis=1).astype(BF16)
    woa = w_o[:A_WIDTH].astype(BF16)
    wob = jnp.pad(w_o[A_WIDTH:A_WIDTH + B_WIDTH].reshape(FOX_HEADS, FOX_DIM, D_MODEL),
                  ((0, 0), (0, 64), (0, 0))).reshape(512, D_MODEL).astype(BF16)
    woc = w_o[A_WIDTH + B_WIDTH:].astype(BF16)
    ga = out_norm[:A_WIDTH].reshape(1, A_WIDTH)
    gb = jnp.pad(out_norm[A_WIDTH:A_WIDTH + B_WIDTH].reshape(FOX_HEADS, FOX_DIM),
                 ((0, 0), (0, 64))).reshape(1, 512)
    gc = out_norm[A_WIDTH + B_WIDTH:].reshape(1, C_WIDTH)
    nblk = FFN_HIDDEN // FFN_BLOCK
    wg = w_gu[:, :FFN_HIDDEN].reshape(D_MODEL, nblk, FFN_BLOCK).transpose(1, 0, 2).astype(BF16)
    wu = w_gu[:, FFN_HIDDEN:].reshape(D_MODEL, nblk, FFN_BLOCK).transpose(1, 0, 2).astype(BF16)
    wd = w_down.reshape(nblk, FFN_BLOCK, D_MODEL).astype(BF16)
    fb = _pad_cols(f_bias.reshape(1, FOX_HEADS), LANES)
    return w_in_r, wuq, wukv_r, woa, wob, woc, ga, gb, gc, wg, wu, wd, fb


def _rope_tables(seq):
    pos = jnp.arange(seq, dtype=F32)
    inv = ROPE_THETA ** (-jnp.arange(0, MLA_ROPE, 2, dtype=F32) / MLA_ROPE)
    ang = pos[:, None] * inv[None, :]
    cos, sin = jnp.cos(ang), jnp.sin(ang)
    z = lambda n: jnp.zeros((seq, n), F32)
    half = MLA_ROPE // 2
    tcos = jnp.concatenate([z(MLA_NOPE), cos, cos, z(32)], axis=1)
    ts1 = jnp.concatenate([z(MLA_NOPE), -sin, z(half), z(32)], axis=1)
    ts2 = jnp.concatenate([z(MLA_NOPE), z(half), sin, z(32)], axis=1)
    return tcos, ts1, ts2


def _tile_bias(bias):
    nq = CHK_TILE // CHUNK
    rows = [jnp.pad(bias, ((0, 0), (0, 0), (c * CHUNK, CHK_WIN - BAND - c * CHUNK)),
                    constant_values=NEG) for c in range(nq)]
    return jnp.concatenate(rows, axis=1)


def kernel(x, mem, norm_mix, w_in, q_norm, w_uq, kv_norm, w_ukv, f_bias, rel_bias, out_norm, w_o,
           norm_cross, norm_mem, w_cq, w_ckv, w_co, norm_ffn, w_gu, w_down, final_norm):
    b, s, _ = x.shape
    depth = w_in.shape[0]
    tcos, ts1, ts2 = _rope_tables(s)
    tri = jnp.tril(jnp.ones((TOK_TILE, TOK_TILE), BF16))
    row = lambda v: v.reshape(1, -1)
    for l in range(depth):
        (w_in_r, wuq, wukv_r, woa, wob, woc, ga, gb, gc, wg, wu, wd, fb) = _layer_weights(
            w_in[l], w_uq[l], w_ukv[l], w_o[l], out_norm[l], w_gu[l], w_down[l], f_bias[l])
        mq, mk, mv, fq, fk, fv, cq, ck, cv, cum = _in_proj(
            x, row(norm_mix[l]), w_in_r, row(q_norm[l]), wuq, row(kv_norm[l]), wukv_r,
            tcos, ts1, ts2, fb, tri)
        ya = _sweep(mq, mk, mv, chunk_mask=True)
        cum_h = jnp.transpose(cum[:, :, :FOX_HEADS], (0, 2, 1))
        yb = _sweep(fq, fk, fv, cum_h.reshape(b, FOX_HEADS, s, 1),
                    cum_h.reshape(b, FOX_HEADS, s // ATT_TILE, ATT_TILE), chunk_mask=False)
        pad = ((0, 0), (CHK_LEFT * CHUNK, 0), (0, 0))
        yc = _chunk_attn(cq, jnp.pad(ck, pad), jnp.pad(cv, pad), _tile_bias(_rel_bias(rel_bias[l])))
        km, vm = _mem_kv(mem, row(norm_mem[l]), w_ckv[l].astype(BF16))
        x = _mix_cross(x, ya, yb, yc, ga, gb, gc, woa, wob, woc, row(norm_cross[l]),
                       w_cq[l].astype(BF16), km, vm, w_co[l].astype(BF16))
        x = _ffn(x, row(norm_ffn[l]), wg, wu, wd, row(final_norm), final_norm=(l == depth - 1))
    return x
```

```python
import functools
import math

import jax
import jax.numpy as jnp
from jax import lax
from jax.experimental import pallas as pl
from jax.experimental.pallas import tpu as pltpu

D_MODEL = 1024
CHUNK = 64
MEM_LEN = 256
EPS = 1e-6
MLA_HEADS = 4
MLA_Q_RANK = 256
MLA_KV_RANK = 128
MLA_NOPE = 64
MLA_ROPE = 32
MLA_V = 128
ROPE_THETA = 10000.0
FOX_HEADS = 4
FOX_DIM = 64
CHK_HEADS = 4
CHK_DIM = 64
CHK_LEFT = 8
BAND = (CHK_LEFT + 1) * CHUNK
REL_MAX = 128
REL_SIZE = (CHUNK - 1) + REL_MAX + 1
A_WIDTH = MLA_HEADS * MLA_V
B_WIDTH = FOX_HEADS * FOX_DIM
C_WIDTH = CHK_HEADS * CHK_DIM
CROSS_HEADS = 4
CROSS_DIM = 128
CROSS_WIDTH = CROSS_HEADS * CROSS_DIM
FFN_HIDDEN = 2816

LANES = 128
LOG2E = 1.4426950408889634
NEG = -1e30
VMEM_LIMIT = 56 * 1024 * 1024

TOK_TILE = 512
ATT_TILE = 512
CHK_TILE = 256
CHK_WIN = CHK_TILE + CHK_LEFT * CHUNK
FFN_BLOCK = 256
IN_COLS = 2048

BF16 = jnp.bfloat16
F32 = jnp.float32
NT_DIMS = (((1,), (1,)), ((), ()))


def _params(*semantics):
    return pltpu.CompilerParams(dimension_semantics=semantics, vmem_limit_bytes=VMEM_LIMIT)


def _rms(xf, gain, width):
    return xf * lax.rsqrt(jnp.sum(xf * xf, axis=-1, keepdims=True) * (1.0 / width) + EPS) * gain


def _lane_iota(shape):
    return lax.broadcasted_iota(jnp.int32, shape, len(shape) - 1)


def _const_spec(shape):
    zeros = (0,) * len(shape)
    return pl.BlockSpec(shape, lambda *_: zeros)


def _in_proj_kernel(x_ref, g_ref, w_ref, qn_ref, wuq_ref, kvn_ref, wukv_ref, tc_ref, ts1_ref,
                    ts2_ref, fb_ref, tri_ref,
                    mq_ref, mk_ref, mv_ref, fq_ref, fk_ref, fv_ref, cq_ref, ck_ref, cv_ref,
                    cum_ref, carry_ref):
    i = pl.program_id(1)
    tm = x_ref.shape[1]
    h = _rms(x_ref[0], g_ref[...], D_MODEL).astype(BF16)

    def proj(lo, hi):
        return jnp.dot(h, w_ref[:, lo:hi], preferred_element_type=F32)

    lane = _lane_iota((tm, LANES))
    tcos, ts1, ts2 = tc_ref[...], ts1_ref[...], ts2_ref[...]

    def rope(y, keep_low):
        base = tcos + jnp.where(lane < MLA_NOPE, 1.0, 0.0) if keep_low else tcos
        return (y * base + pltpu.roll(y, LANES - MLA_ROPE // 2, 1) * ts1
                + pltpu.roll(y, MLA_ROPE // 2, 1) * ts2)

    c_q = proj(0, 256)
    q_up = jnp.dot(_rms(c_q, qn_ref[...], MLA_Q_RANK).astype(BF16), wuq_ref[...],
                   preferred_element_type=F32)
    misc = proj(384, 512)
    k_pe = rope(misc, False)
    c_kv = proj(256, 384)
    kv_up = jnp.dot(_rms(c_kv, kvn_ref[...], MLA_KV_RANK).astype(BF16), wukv_ref[...],
                    preferred_element_type=F32)
    q_scale = (MLA_NOPE + MLA_ROPE) ** -0.5 * LOG2E
    for hd in range(MLA_HEADS):
        sl = slice(hd * LANES, (hd + 1) * LANES)
        mq_ref[0, hd] = (rope(q_up[:, sl], True) * q_scale).astype(BF16)
        mk_ref[0, hd] = (kv_up[:, sl] + k_pe).astype(BF16)
        mv_ref[0, hd] = kv_up[:, A_WIDTH + hd * LANES:A_WIDTH + (hd + 1) * LANES].astype(BF16)

    def head_block(y, hd):
        blk = y[:, (hd // 2) * LANES:(hd // 2 + 1) * LANES]
        return pltpu.roll(blk, FOX_DIM, 1) if hd % 2 else blk

    fq = proj(512, 768)
    fk = proj(768, 1024)
    fv = proj(1024, 1280)
    f_scale = FOX_DIM ** -0.5 * LOG2E
    ones_col = jnp.where(lane == FOX_DIM, 1.0, 0.0)
    for hd in range(FOX_HEADS):
        fq_ref[0, hd] = jnp.where(lane < FOX_DIM, head_block(fq, hd) * f_scale, 0.0).astype(BF16)
        fk_ref[0, hd] = jnp.where(lane < FOX_DIM, head_block(fk, hd), 0.0).astype(BF16)
        fv_ref[0, hd] = jnp.where(lane < FOX_DIM, head_block(fv, hd), ones_col).astype(BF16)

    z = misc + fb_ref[...]
    log_f = jnp.minimum(z, 0.0) - jnp.log1p(jnp.exp(-jnp.abs(z)))
    log_f = jnp.where(lane < FOX_HEADS, log_f, 0.0)
    p_hi = log_f.astype(BF16)
    rem = log_f - p_hi.astype(F32)
    p_mid = rem.astype(BF16)
    p_lo = (rem - p_mid.astype(F32)).astype(BF16)
    tri = tri_ref[...]
    cum = (jnp.dot(tri, p_hi, preferred_element_type=F32)
           + jnp.dot(tri, p_mid, preferred_element_type=F32)
           + jnp.dot(tri, p_lo, preferred_element_type=F32))

    @pl.when(i == 0)
    def _():
        carry_ref[...] = jnp.zeros_like(carry_ref)

    cum = cum + carry_ref[...]
    carry_ref[...] = cum[tm - 1:tm, :]
    cum_ref[0] = cum * LOG2E

    cq_ref[0] = (proj(1280, 1536) * (CHK_DIM ** -0.5 * LOG2E)).astype(BF16)
    ck_ref[0] = proj(1536, 1792).astype(BF16)
    cv_ref[0] = proj(1792, 2048).astype(BF16)


def _in_proj(x, g, w, qn, wuq, kvn, wukv, tcos, ts1, ts2, fb, tri):
    b, s, _ = x.shape
    tm = TOK_TILE
    head_spec = pl.BlockSpec((1, 4, tm, LANES), lambda bi, i: (bi, 0, i, 0))
    dense_spec = pl.BlockSpec((1, tm, 256), lambda bi, i: (bi, i, 0))
    tab_spec = pl.BlockSpec((tm, LANES), lambda bi, i: (i, 0))
    head_shape = jax.ShapeDtypeStruct((b, 4, s, LANES), BF16)
    dense_shape = jax.ShapeDtypeStruct((b, s, 256), BF16)
    return pl.pallas_call(
        _in_proj_kernel,
        grid=(b, s // tm),
        in_specs=[pl.BlockSpec((1, tm, D_MODEL), lambda bi, i: (bi, i, 0)),
                  _const_spec((1, D_MODEL)), _const_spec((D_MODEL, IN_COLS)),
                  _const_spec((1, MLA_Q_RANK)), _const_spec((MLA_Q_RANK, 512)),
                  _const_spec((1, MLA_KV_RANK)), _const_spec((MLA_KV_RANK, 1024)),
                  tab_spec, tab_spec, tab_spec, _const_spec((1, LANES)), _const_spec((tm, tm))],
        out_specs=[head_spec] * 6 + [dense_spec] * 3
                  + [pl.BlockSpec((1, tm, LANES), lambda bi, i: (bi, i, 0))],
        out_shape=[head_shape] * 6 + [dense_shape] * 3
                  + [jax.ShapeDtypeStruct((b, s, LANES), F32)],
        scratch_shapes=[pltpu.VMEM((1, LANES), F32)],
        compiler_params=_params("arbitrary", "arbitrary"),
        name="in_proj",
    )(x, g, w, qn, wuq, kvn, wukv, tcos, ts1, ts2, fb, tri)


def _sweep_kernel(*refs, decay, chunk_mask):
    if decay:
        q_ref, k_ref, v_ref, cq_ref, ck_ref, o_ref, m_sc, l_sc, acc_sc, s_sc = refs
    else:
        q_ref, k_ref, v_ref, o_ref, m_sc, l_sc, acc_sc, s_sc = refs
    i = pl.program_id(2)
    t = ATT_TILE
    q = q_ref[0, 0]
    m_sc[...] = jnp.full_like(m_sc, NEG)
    l_sc[...] = jnp.zeros_like(l_sc)
    acc_sc[...] = jnp.zeros_like(acc_sc)
    if decay:
        cq = cq_ref[0, 0]
        cq_rep = jnp.broadcast_to(cq, (t, LANES))

    def scores(j, slot):
        off = pl.multiple_of(j * t, t)
        k = k_ref[0, 0, pl.ds(off, t), :]
        s_sc[slot] = lax.dot_general(q, k, NT_DIMS, preferred_element_type=F32)

    def attend(j, slot, masked):
        off = pl.multiple_of(j * t, t)
        v = v_ref[0, 0, pl.ds(off, t), :]
        s = s_sc[slot]
        if decay:
            s = s - ck_ref[0, 0, pl.ds(j, 1), :]
        if masked:
            qi = lax.broadcasted_iota(jnp.int32, (t, t), 0)
            ki = lax.broadcasted_iota(jnp.int32, (t, t), 1)
            if chunk_mask:
                qi, ki = qi // CHUNK, ki // CHUNK
            s = jnp.where(ki <= qi, s, NEG)
        m_old = m_sc[...]
        m_cur = jnp.max(s, axis=-1, keepdims=True)
        if decay:
            m_cur = m_cur + cq_rep
        m_new = jnp.maximum(m_old, m_cur)
        alpha = jnp.exp2(m_old - m_new)
        shift = m_new - cq_rep if decay else m_new
        ps = [jnp.exp2(s[:, c * LANES:(c + 1) * LANES] - shift) for c in range(t // LANES)]
        if not decay:
            l_sc[...] = alpha * l_sc[...] + functools.reduce(lambda a, b: a + b, ps)
        p = jnp.concatenate(ps, axis=-1).astype(BF16)
        acc_sc[...] = alpha * acc_sc[...] + jnp.dot(p, v, preferred_element_type=F32)
        m_sc[...] = m_new

    scores(0, 0)

    def body(jj, carry):
        scores(2 * jj + 1, 1)
        attend(2 * jj, 0, False)
        scores(2 * jj + 2, 0)
        attend(2 * jj + 1, 1, False)
        return carry

    lax.fori_loop(0, i // 2, body, 0)

    @pl.when(i % 2 == 1)
    def _():
        scores(i, 1)
        attend(i - 1, 0, False)
        attend(i, 1, True)

    @pl.when(i % 2 == 0)
    def _():
        attend(i, 0, True)

    acc = acc_sc[...]
    if decay:
        denom = acc[:, FOX_DIM:FOX_DIM + 1]
    else:
        denom = jnp.sum(l_sc[...], axis=-1, keepdims=True)
    o_ref[0, 0] = acc / denom


def _sweep(q, k, v, cq=None, ck=None, *, chunk_mask):
    b, nh, s, _ = q.shape
    t = ATT_TILE
    decay = cq is not None
    q_spec = pl.BlockSpec((1, 1, t, LANES), lambda bi, hi, i: (bi, hi, i, 0))
    kv_spec = pl.BlockSpec((1, 1, s, LANES), lambda bi, hi, i: (bi, hi, 0, 0))
    in_specs = [q_spec, kv_spec, kv_spec]
    args = [q, k, v]
    if decay:
        in_specs += [pl.BlockSpec((1, 1, t, 1), lambda bi, hi, i: (bi, hi, i, 0)),
                     pl.BlockSpec((1, 1, s // t, t), lambda bi, hi, i: (bi, hi, 0, 0))]
        args += [cq, ck]
    return pl.pallas_call(
        functools.partial(_sweep_kernel, decay=decay, chunk_mask=chunk_mask),
        grid=(b, nh, s // t),
        in_specs=in_specs,
        out_specs=q_spec,
        out_shape=jax.ShapeDtypeStruct((b, nh, s, LANES), F32),
        scratch_shapes=[pltpu.VMEM((t, LANES), F32), pltpu.VMEM((t, LANES), F32),
                        pltpu.VMEM((t, LANES), F32), pltpu.VMEM((2, t, t), F32)],
        compiler_params=_params("arbitrary", "arbitrary", "arbitrary"),
        name="fox_sweep" if decay else "mla_sweep",
    )(*args)


def _rel_bias_kernel(tab_ref, o_ref):
    qi = lax.broadcasted_iota(jnp.int32, (CHUNK, BAND), 0)
    ki = lax.broadcasted_iota(jnp.int32, (CHUNK, BAND), 1)
    idx = jnp.clip(qi + CHK_LEFT * CHUNK - ki, -(CHUNK - 1), REL_MAX) + (CHUNK - 1)
    for hd in range(CHK_HEADS):
        def body(r, acc):
            return jnp.where(idx == r, tab_ref[hd, r], acc)
        o_ref[hd] = lax.fori_loop(0, REL_SIZE, body, jnp.zeros((CHUNK, BAND), F32)) * LOG2E


def _rel_bias(table):
    return pl.pallas_call(
        _rel_bias_kernel,
        in_specs=[pl.BlockSpec(memory_space=pltpu.SMEM)],
        out_specs=pl.BlockSpec(memory_space=pltpu.VMEM),
        out_shape=jax.ShapeDtypeStruct((CHK_HEADS, CHUNK, BAND), F32),
        name="rel_bias",
    )(table)


def _chunk_kernel(q_ref, k_ref, v_ref, bias_ref, o_ref):
    i = pl.program_id(1)
    start = pl.multiple_of(i * CHK_TILE, CHK_TILE)
    q = q_ref[0]
    kw = k_ref[0, pl.ds(start, CHK_WIN), :]
    vw = v_ref[0, pl.ds(start, CHK_WIN), :]
    col = _lane_iota((CHK_TILE, CHK_WIN))
    valid = col >= CHK_LEFT * CHUNK - start
    outs = []
    for hd in range(CHK_HEADS):
        sl = slice(hd * CHK_DIM, (hd + 1) * CHK_DIM)
        s = lax.dot_general(q[:, sl], kw[:, sl], NT_DIMS, preferred_element_type=F32)
        s = jnp.where(valid, s + bias_ref[hd], NEG)
        m = jnp.max(s, axis=-1, keepdims=True)
        p = jnp.exp2(s - m)
        denom = jnp.sum(p, axis=-1, keepdims=True)
        outs.append(jnp.dot(p.astype(BF16), vw[:, sl], preferred_element_type=F32) / denom)
    o_ref[0] = jnp.concatenate(outs, axis=-1)


def _chunk_attn(q, kp, vp, bias):
    b, s, _ = q.shape
    sp = kp.shape[1]
    return pl.pallas_call(
        _chunk_kernel,
        grid=(b, s // CHK_TILE),
        in_specs=[pl.BlockSpec((1, CHK_TILE, C_WIDTH), lambda bi, i: (bi, i, 0)),
                  pl.BlockSpec((1, sp, C_WIDTH), lambda bi, i: (bi, 0, 0)),
                  pl.BlockSpec((1, sp, C_WIDTH), lambda bi, i: (bi, 0, 0)),
                  _const_spec((CHK_HEADS, CHK_TILE, CHK_WIN))],
        out_specs=pl.BlockSpec((1, CHK_TILE, C_WIDTH), lambda bi, i: (bi, i, 0)),
        out_shape=jax.ShapeDtypeStruct((b, s, C_WIDTH), F32),
        compiler_params=_params("arbitrary", "arbitrary"),
        name="chunk_attn",
    )(q, kp, vp, bias)


def _mem_kernel(m_ref, g_ref, w_ref, k_ref, v_ref):
    h = _rms(m_ref[0], g_ref[...], D_MODEL).astype(BF16)
    kv = jnp.dot(h, w_ref[...], preferred_element_type=F32)
    k_ref[0] = kv[:, :CROSS_WIDTH].astype(BF16)
    v_ref[0] = kv[:, CROSS_WIDTH:].astype(BF16)


def _mem_kv(mem, g, w):
    b = mem.shape[0]
    spec = pl.BlockSpec((1, MEM_LEN, CROSS_WIDTH), lambda bi: (bi, 0, 0))
    shape = jax.ShapeDtypeStruct((b, MEM_LEN, CROSS_WIDTH), BF16)
    return pl.pallas_call(
        _mem_kernel,
        grid=(b,),
        in_specs=[pl.BlockSpec((1, MEM_LEN, D_MODEL), lambda bi: (bi, 0, 0)),
                  _const_spec((1, D_MODEL)), _const_spec((D_MODEL, 2 * CROSS_WIDTH))],
        out_specs=[spec, spec],
        out_shape=[shape, shape],
        compiler_params=_params("arbitrary"),
        name="mem_kv",
    )(mem, g, w)


def _mix_cross_kernel(x_ref, ya_ref, yb_ref, yc_ref, ga_ref, gb_ref, gc_ref, woa_ref, wob_ref,
                      woc_ref, gx_ref, wcq_ref, km_ref, vm_ref, wco_ref, o_ref):
    tm = x_ref.shape[1]
    lane = _lane_iota((tm, LANES))
    ya = jnp.concatenate([ya_ref[0, hd] for hd in range(MLA_HEADS)], axis=-1)
    yb = jnp.concatenate([jnp.where(lane < FOX_DIM, yb_ref[0, hd], 0.0)
                          for hd in range(FOX_HEADS)], axis=-1)
    x1 = (x_ref[0]
          + jnp.dot(_rms(ya, ga_ref[...], A_WIDTH).astype(BF16), woa_ref[...],
                    preferred_element_type=F32)
          + jnp.dot(_rms(yb, gb_ref[...], B_WIDTH).astype(BF16), wob_ref[...],
                    preferred_element_type=F32)
          + jnp.dot(_rms(yc_ref[0], gc_ref[...], C_WIDTH).astype(BF16), woc_ref[...],
                    preferred_element_type=F32))
    h = _rms(x1, gx_ref[...], D_MODEL).astype(BF16)
    q = (jnp.dot(h, wcq_ref[...], preferred_element_type=F32)
         * (CROSS_DIM ** -0.5 * LOG2E)).astype(BF16)
    outs = []
    for hd in range(CROSS_HEADS):
        sl = slice(hd * CROSS_DIM, (hd + 1) * CROSS_DIM)
        s = lax.dot_general(q[:, sl], km_ref[0, :, sl], NT_DIMS, preferred_element_type=F32)
        m = jnp.max(s, axis=-1, keepdims=True)
        p = jnp.exp2(s - m)
        denom = jnp.sum(p, axis=-1, keepdims=True)
        outs.append(jnp.dot(p.astype(BF16), vm_ref[0, :, sl], preferred_element_type=F32)
                    / denom)
    o = jnp.concatenate(outs, axis=-1).astype(BF16)
    o_ref[0] = x1 + jnp.dot(o, wco_ref[...], preferred_element_type=F32)


def _mix_cross(x, ya, yb, yc, ga, gb, gc, woa, wob, woc, gx, wcq, km, vm, wco):
    b, s, _ = x.shape
    tm = TOK_TILE
    x_spec = pl.BlockSpec((1, tm, D_MODEL), lambda bi, i: (bi, i, 0))
    head_spec = pl.BlockSpec((1, 4, tm, LANES), lambda bi, i: (bi, 0, i, 0))
    mem_spec = pl.BlockSpec((1, MEM_LEN, CROSS_WIDTH), lambda bi, i: (bi, 0, 0))
    return pl.pallas_call(
        _mix_cross_kernel,
        grid=(b, s // tm),
        in_specs=[x_spec, head_spec, head_spec,
                  pl.BlockSpec((1, tm, C_WIDTH), lambda bi, i: (bi, i, 0)),
                  _const_spec((1, A_WIDTH)), _const_spec((1, 512)), _const_spec((1, C_WIDTH)),
                  _const_spec((A_WIDTH, D_MODEL)), _const_spec((512, D_MODEL)),
                  _const_spec((C_WIDTH, D_MODEL)), _const_spec((1, D_MODEL)),
                  _const_spec((D_MODEL, CROSS_WIDTH)), mem_spec, mem_spec,
                  _const_spec((CROSS_WIDTH, D_MODEL))],
        out_specs=x_spec,
        out_shape=jax.ShapeDtypeStruct(x.shape, F32),
        compiler_params=_params("arbitrary", "arbitrary"),
        name="mix_cross",
    )(x, ya, yb, yc, ga, gb, gc, woa, wob, woc, gx, wcq, km, vm, wco)


def _ffn_kernel(x_ref, g_ref, wg_ref, wu_ref, wd_ref, gf_ref, o_ref, acc_ref, *, final_norm):
    x = x_ref[0]
    h = _rms(x, g_ref[...], D_MODEL).astype(BF16)
    acc_ref[...] = x

    def body(c, carry):
        gate = jnp.dot(h, wg_ref[c], preferred_element_type=F32)
        up = jnp.dot(h, wu_ref[c], preferred_element_type=F32)
        act = (gate * jax.nn.sigmoid(gate) * up).astype(BF16)
        acc_ref[...] += jnp.dot(act, wd_ref[c], preferred_element_type=F32)
        return carry

    lax.fori_loop(0, FFN_HIDDEN // FFN_BLOCK, body, 0)
    y = acc_ref[...]
    o_ref[0] = _rms(y, gf_ref[...], D_MODEL) if final_norm else y


def _ffn(x, g, wg, wu, wd, gf, *, final_norm):
    b, s, _ = x.shape
    tm = TOK_TILE
    nblk = FFN_HIDDEN // FFN_BLOCK
    x_spec = pl.BlockSpec((1, tm, D_MODEL), lambda bi, i: (bi, i, 0))

    def resident(shape):
        return pl.BlockSpec(shape, lambda bi, i: (0, 0, 0), pipeline_mode=pl.Buffered(1))

    return pl.pallas_call(
        functools.partial(_ffn_kernel, final_norm=final_norm),
        grid=(b, s // tm),
        in_specs=[x_spec, _const_spec((1, D_MODEL)),
                  resident((nblk, D_MODEL, FFN_BLOCK)), resident((nblk, D_MODEL, FFN_BLOCK)),
                  resident((nblk, FFN_BLOCK, D_MODEL)), _const_spec((1, D_MODEL))],
        out_specs=x_spec,
        out_shape=jax.ShapeDtypeStruct(x.shape, F32),
        scratch_shapes=[pltpu.VMEM((tm, D_MODEL), F32)],
        compiler_params=_params("arbitrary", "arbitrary"),
        name="ffn",
    )(x, g, wg, wu, wd, gf)


def _pad_cols(a, width):
    return jnp.pad(a, ((0, 0), (0, width - a.shape[1])))


def _layer_weights(w_in, w_uq, w_ukv, w_o, out_norm, w_gu, w_down, f_bias):
    sizes = (256, 128, 32, 256, 256, 256, 4, 256, 256, 256)
    parts, at = [], 0
    for n in sizes:
        parts.append(w_in[:, at:at + n])
        at += n
    c_q, c_kv, k_rope, fq, fk, fv, f_logit, cq, ck, cv = parts
    zeros = lambda n: jnp.zeros((D_MODEL, n), F32)
    misc = jnp.concatenate([f_logit, zeros(MLA_NOPE - FOX_HEADS), k_rope, zeros(32)], axis=1)
    w_in_r = jnp.concatenate([c_q, c_kv, misc, fq, fk, fv, cq, ck, cv], axis=1).astype(BF16)
    wuq = jnp.pad(w_uq.reshape(MLA_Q_RANK, MLA_HEADS, MLA_NOPE + MLA_ROPE),
                  ((0, 0), (0, 0), (0, 32))).reshape(MLA_Q_RANK, 512).astype(BF16)
    wukv = w_ukv.reshape(MLA_KV_RANK, MLA_HEADS, MLA_NOPE + MLA_V)
    wk = jnp.pad(wukv[:, :, :MLA_NOPE], ((0, 0), (0, 0), (0, 64))).reshape(MLA_KV_RANK, 512)
    wv = wukv[:, :, MLA_NOPE:].reshape(MLA_KV_RANK, 512)
    wukv_r = jnp.concatenate([wk, wv], axis=1).astype(BF16)
    woa = w_o[:A_WIDTH].astype(BF16)
    wob = jnp.pad(w_o[A_WIDTH:A_WIDTH + B_WIDTH].reshape(FOX_HEADS, FOX_DIM, D_MODEL),
                  ((0, 0), (0, 64), (0, 0))).reshape(512, D_MODEL).astype(BF16)
    woc = w_o[A_WIDTH + B_WIDTH:].astype(BF16)
    ga = out_norm[:A_WIDTH].reshape(1, A_WIDTH)
    gb = jnp.pad(out_norm[A_WIDTH:A_WIDTH + B_WIDTH].reshape(FOX_HEADS, FOX_DIM),
                 ((0, 0), (0, 64))).reshape(1, 512)
    gc = out_norm[A_WIDTH + B_WIDTH:].reshape(1, C_WIDTH)
    nblk = FFN_HIDDEN // FFN_BLOCK
    wg = w_gu[:, :FFN_HIDDEN].reshape(D_MODEL, nblk, FFN_BLOCK).transpose(1, 0, 2).astype(BF16)
    wu = w_gu[:, FFN_HIDDEN:].reshape(D_MODEL, nblk, FFN_BLOCK).transpose(1, 0, 2).astype(BF16)
    wd = w_down.reshape(nblk, FFN_BLOCK, D_MODEL).astype(BF16)
    fb = _pad_cols(f_bias.reshape(1, FOX_HEADS), LANES)
    return w_in_r, wuq, wukv_r, woa, wob, woc, ga, gb, gc, wg, wu, wd, fb


def _rope_tables(seq):
    pos = jnp.arange(seq, dtype=F32)
    inv = ROPE_THETA ** (-jnp.arange(0, MLA_ROPE, 2, dtype=F32) / MLA_ROPE)
    ang = pos[:, None] * inv[None, :]
    cos, sin = jnp.cos(ang), jnp.sin(ang)
    z = lambda n: jnp.zeros((seq, n), F32)
    half = MLA_ROPE // 2
    tcos = jnp.concatenate([z(MLA_NOPE), cos, cos, z(32)], axis=1)
    ts1 = jnp.concatenate([z(MLA_NOPE), -sin, z(half), z(32)], axis=1)
    ts2 = jnp.concatenate([z(MLA_NOPE), z(half), sin, z(32)], axis=1)
    return tcos, ts1, ts2


def _tile_bias(bias):
    nq = CHK_TILE // CHUNK
    rows = [jnp.pad(bias, ((0, 0), (0, 0), (c * CHUNK, CHK_WIN - BAND - c * CHUNK)),
                    constant_values=NEG) for c in range(nq)]
    return jnp.concatenate(rows, axis=1)


def kernel(x, mem, norm_mix, w_in, q_norm, w_uq, kv_norm, w_ukv, f_bias, rel_bias, out_norm, w_o,
           norm_cross, norm_mem, w_cq, w_ckv, w_co, norm_ffn, w_gu, w_down, final_norm):
    b, s, _ = x.shape
    depth = w_in.shape[0]
    tcos, ts1, ts2 = _rope_tables(s)
    tri = jnp.tril(jnp.ones((TOK_TILE, TOK_TILE), BF16))
    row = lambda v: v.reshape(1, -1)
    for l in range(depth):
        (w_in_r, wuq, wukv_r, woa, wob, woc, ga, gb, gc, wg, wu, wd, fb) = _layer_weights(
            w_in[l], w_uq[l], w_ukv[l], w_o[l], out_norm[l], w_gu[l], w_down[l], f_bias[l])
        mq, mk, mv, fq, fk, fv, cq, ck, cv, cum = _in_proj(
            x, row(norm_mix[l]), w_in_r, row(q_norm[l]), wuq, row(kv_norm[l]), wukv_r,
            tcos, ts1, ts2, fb, tri)
        ya = _sweep(mq, mk, mv, chunk_mask=True)
        cum_h = jnp.transpose(cum[:, :, :FOX_HEADS], (0, 2, 1))
        yb = _sweep(fq, fk, fv, cum_h.reshape(b, FOX_HEADS, s, 1),
                    cum_h.reshape(b, FOX_HEADS, s // ATT_TILE, ATT_TILE), chunk_mask=False)
        pad = ((0, 0), (CHK_LEFT * CHUNK, 0), (0, 0))
        yc = _chunk_attn(cq, jnp.pad(ck, pad), jnp.pad(cv, pad), _tile_bias(_rel_bias(rel_bias[l])))
        km, vm = _mem_kv(mem, row(norm_mem[l]), w_ckv[l].astype(BF16))
        x = _mix_cross(x, ya, yb, yc, ga, gb, gc, woa, wob, woc, row(norm_cross[l]),
                       w_cq[l].astype(BF16), km, vm, w_co[l].astype(BF16))
        x = _ffn(x, row(norm_ffn[l]), wg, wu, wd, row(final_norm), final_norm=(l == depth - 1))
    return x
```

```python
import functools
import math

import jax
import jax.numpy as jnp
from jax import lax
from jax.experimental import pallas as pl
from jax.experimental.pallas import tpu as pltpu

D_MODEL = 1024
CHUNK = 64
MEM_LEN = 256
EPS = 1e-6
MLA_HEADS = 4
MLA_Q_RANK = 256
MLA_KV_RANK = 128
MLA_NOPE = 64
MLA_ROPE = 32
MLA_V = 128
ROPE_THETA = 10000.0
FOX_HEADS = 4
FOX_DIM = 64
CHK_HEADS = 4
CHK_DIM = 64
CHK_LEFT = 8
BAND = (CHK_LEFT + 1) * CHUNK
REL_MAX = 128
REL_SIZE = (CHUNK - 1) + REL_MAX + 1
A_WIDTH = MLA_HEADS * MLA_V
B_WIDTH = FOX_HEADS * FOX_DIM
C_WIDTH = CHK_HEADS * CHK_DIM
CROSS_HEADS = 4
CROSS_DIM = 128
CROSS_WIDTH = CROSS_HEADS * CROSS_DIM
FFN_HIDDEN = 2816

LANES = 128
LOG2E = 1.4426950408889634
NEG = -1e30
SKIP_LOG2 = 152.0
VMEM_LIMIT = 56 * 1024 * 1024

TOK_TILE = 512
ATT_TILE = 512
CHK_TILE = 256
CHK_WIN = CHK_TILE + CHK_LEFT * CHUNK
FFN_BLOCK = 256
IN_COLS = 2048

BF16 = jnp.bfloat16
F32 = jnp.float32
NT_DIMS = (((1,), (1,)), ((), ()))


def _params(*semantics):
    return pltpu.CompilerParams(dimension_semantics=semantics, vmem_limit_bytes=VMEM_LIMIT)


def _rms(xf, gain, width):
    return xf * lax.rsqrt(jnp.sum(xf * xf, axis=-1, keepdims=True) * (1.0 / width) + EPS) * gain


def _lane_iota(shape):
    return lax.broadcasted_iota(jnp.int32, shape, len(shape) - 1)


def _const_spec(shape):
    zeros = (0,) * len(shape)
    return pl.BlockSpec(shape, lambda *_: zeros)


def _in_proj_kernel(x_ref, g_ref, w_ref, qn_ref, wuq_ref, kvn_ref, wukv_ref, tc_ref, ts1_ref,
                    ts2_ref, fb_ref, tri_ref,
                    mq_ref, mk_ref, mv_ref, fq_ref, fk_ref, fv_ref, cq_ref, ck_ref, cv_ref,
                    cum_ref, carry_ref):
    i = pl.program_id(1)
    tm = x_ref.shape[1]
    h = _rms(x_ref[0], g_ref[...], D_MODEL).astype(BF16)

    def proj(lo, hi):
        return jnp.dot(h, w_ref[:, lo:hi], preferred_element_type=F32)

    lane = _lane_iota((tm, LANES))
    tcos, ts1, ts2 = tc_ref[...], ts1_ref[...], ts2_ref[...]

    def rope(y, keep_low):
        base = tcos + jnp.where(lane < MLA_NOPE, 1.0, 0.0) if keep_low else tcos
        return (y * base + pltpu.roll(y, LANES - MLA_ROPE // 2, 1) * ts1
                + pltpu.roll(y, MLA_ROPE // 2, 1) * ts2)

    c_q = proj(0, 256)
    q_up = jnp.dot(_rms(c_q, qn_ref[...], MLA_Q_RANK).astype(BF16), wuq_ref[...],
                   preferred_element_type=F32)
    misc = proj(384, 512)
    k_pe = rope(misc, False)
    c_kv = proj(256, 384)
    kv_up = jnp.dot(_rms(c_kv, kvn_ref[...], MLA_KV_RANK).astype(BF16), wukv_ref[...],
                    preferred_element_type=F32)
    q_scale = (MLA_NOPE + MLA_ROPE) ** -0.5 * LOG2E
    for hd in range(MLA_HEADS):
        sl = slice(hd * LANES, (hd + 1) * LANES)
        mq_ref[0, hd] = (rope(q_up[:, sl], True) * q_scale).astype(BF16)
        mk_ref[0, hd] = (kv_up[:, sl] + k_pe).astype(BF16)
        mv_ref[0, hd] = kv_up[:, A_WIDTH + hd * LANES:A_WIDTH + (hd + 1) * LANES].astype(BF16)

    def head_block(y, hd):
        blk = y[:, (hd // 2) * LANES:(hd // 2 + 1) * LANES]
        return pltpu.roll(blk, FOX_DIM, 1) if hd % 2 else blk

    fq = proj(512, 768)
    fk = proj(768, 1024)
    fv = proj(1024, 1280)
    f_scale = FOX_DIM ** -0.5 * LOG2E
    ones_col = jnp.where(lane == FOX_DIM, 1.0, 0.0)
    for hd in range(FOX_HEADS):
        fq_ref[0, hd] = jnp.where(lane < FOX_DIM, head_block(fq, hd) * f_scale, 0.0).astype(BF16)
        fk_ref[0, hd] = jnp.where(lane < FOX_DIM, head_block(fk, hd), 0.0).astype(BF16)
        fv_ref[0, hd] = jnp.where(lane < FOX_DIM, head_block(fv, hd), ones_col).astype(BF16)

    z = misc + fb_ref[...]
    log_f = jnp.minimum(z, 0.0) - jnp.log1p(jnp.exp(-jnp.abs(z)))
    log_f = jnp.where(lane < FOX_HEADS, log_f, 0.0)
    p_hi = log_f.astype(BF16)
    rem = log_f - p_hi.astype(F32)
    p_mid = rem.astype(BF16)
    p_lo = (rem - p_mid.astype(F32)).astype(BF16)
    parts = jnp.dot(tri_ref[...], jnp.concatenate([p_hi, p_mid, p_lo], axis=-1),
                    preferred_element_type=F32)
    cum = parts[:, :LANES] + parts[:, LANES:2 * LANES] + parts[:, 2 * LANES:]

    @pl.when(i == 0)
    def _():
        carry_ref[...] = jnp.zeros_like(carry_ref)

    cum = cum + carry_ref[...]
    carry_ref[...] = cum[tm - 1:tm, :]
    cum_ref[0] = cum * LOG2E

    cq_ref[0] = (proj(1280, 1536) * (CHK_DIM ** -0.5 * LOG2E)).astype(BF16)
    ck_ref[0] = proj(1536, 1792).astype(BF16)
    cv_ref[0] = proj(1792, 2048).astype(BF16)


def _in_proj(x, g, w, qn, wuq, kvn, wukv, tcos, ts1, ts2, fb, tri):
    b, s, _ = x.shape
    tm = TOK_TILE
    head_spec = pl.BlockSpec((1, 4, tm, LANES), lambda bi, i: (bi, 0, i, 0))
    dense_spec = pl.BlockSpec((1, tm, 256), lambda bi, i: (bi, i, 0))
    tab_spec = pl.BlockSpec((tm, LANES), lambda bi, i: (i, 0))
    head_shape = jax.ShapeDtypeStruct((b, 4, s, LANES), BF16)
    dense_shape = jax.ShapeDtypeStruct((b, s, 256), BF16)
    return pl.pallas_call(
        _in_proj_kernel,
        grid=(b, s // tm),
        in_specs=[pl.BlockSpec((1, tm, D_MODEL), lambda bi, i: (bi, i, 0)),
                  _const_spec((1, D_MODEL)), _const_spec((D_MODEL, IN_COLS)),
                  _const_spec((1, MLA_Q_RANK)), _const_spec((MLA_Q_RANK, 512)),
                  _const_spec((1, MLA_KV_RANK)), _const_spec((MLA_KV_RANK, 1024)),
                  tab_spec, tab_spec, tab_spec, _const_spec((1, LANES)), _const_spec((tm, tm))],
        out_specs=[head_spec] * 6 + [dense_spec] * 3
                  + [pl.BlockSpec((1, tm, LANES), lambda bi, i: (bi, i, 0))],
        out_shape=[head_shape] * 6 + [dense_shape] * 3
                  + [jax.ShapeDtypeStruct((b, s, LANES), F32)],
        scratch_shapes=[pltpu.VMEM((1, LANES), F32)],
        compiler_params=_params("arbitrary", "arbitrary"),
        name="in_proj",
    )(x, g, w, qn, wuq, kvn, wukv, tcos, ts1, ts2, fb, tri)


def _sweep_kernel(*refs, decay, chunk_mask):
    if decay:
        (q_ref, k_ref, v_ref, cq_ref, ck_ref, o_ref,
         m_sc, l_sc, acc_sc, s_sc, kmax_sc, ckmin_sc) = refs
    else:
        q_ref, k_ref, v_ref, o_ref, m_sc, l_sc, acc_sc, s_sc = refs
    i = pl.program_id(2)
    t = ATT_TILE
    q = q_ref[0, 0]
    m_sc[...] = jnp.full_like(m_sc, NEG)
    l_sc[...] = jnp.zeros_like(l_sc)
    acc_sc[...] = jnp.zeros_like(acc_sc)
    if decay:
        cq = cq_ref[0, 0]
        cq_rep = jnp.broadcast_to(cq, (t, LANES))
        nblk = ck_ref.shape[2]

        @pl.when(i == 0)
        def _():
            def key_norm(j, acc):
                kf = k_ref[0, 0, pl.ds(pl.multiple_of(j * t, t), t), :].astype(F32)
                return jnp.maximum(acc, jnp.max(jnp.sum(kf * kf, axis=-1, keepdims=True),
                                                axis=0, keepdims=True))
            kmax_sc[...] = jnp.sqrt(lax.fori_loop(0, nblk, key_norm, jnp.zeros((1, 1), F32)))
            ckmin_sc[...] = jnp.min(ck_ref[0, 0], axis=-1, keepdims=True)

    def first_needed_block():
        if not decay:
            return 0
        qf = q.astype(F32)
        kd = k_ref[0, 0, pl.ds(pl.multiple_of(i * t, t), t), :].astype(F32)
        gap = (jnp.sqrt(jnp.sum(qf * qf, axis=-1, keepdims=True)) * kmax_sc[...]
               - jnp.sum(qf * kd, axis=-1, keepdims=True) + cq)
        gap_max = jnp.max(gap, axis=0, keepdims=True)
        needed = gap_max - ckmin_sc[...] >= -SKIP_LOG2
        blk = lax.broadcasted_iota(jnp.int32, (nblk, 1), 0)
        return jnp.min(jnp.where(needed, blk, i))

    def scores(j, slot):
        off = pl.multiple_of(j * t, t)
        k = k_ref[0, 0, pl.ds(off, t), :]
        s_sc[slot] = lax.dot_general(q, k, NT_DIMS, preferred_element_type=F32)

    def attend(j, slot, masked):
        off = pl.multiple_of(j * t, t)
        v = v_ref[0, 0, pl.ds(off, t), :]
        s = s_sc[slot]
        if decay:
            s = s - ck_ref[0, 0, pl.ds(j, 1), :]
        if masked:
            qi = lax.broadcasted_iota(jnp.int32, (t, t), 0)
            ki = lax.broadcasted_iota(jnp.int32, (t, t), 1)
            if chunk_mask:
                qi, ki = qi // CHUNK, ki // CHUNK
            s = jnp.where(ki <= qi, s, NEG)
        m_old = m_sc[...]
        m_cur = jnp.max(s, axis=-1, keepdims=True)
        if decay:
            m_cur = m_cur + cq_rep
        m_new = jnp.maximum(m_old, m_cur)
        alpha = jnp.exp2(m_old - m_new)
        shift = m_new - cq_rep if decay else m_new
        ps = [jnp.exp2(s[:, c * LANES:(c + 1) * LANES] - shift) for c in range(t // LANES)]
        if not decay:
            l_sc[...] = alpha * l_sc[...] + functools.reduce(lambda a, b: a + b, ps)
        p = jnp.concatenate(ps, axis=-1).astype(BF16)
        acc_sc[...] = alpha * acc_sc[...] + jnp.dot(p, v, preferred_element_type=F32)
        m_sc[...] = m_new

    scores(i, 0)
    first = first_needed_block()
    n_full = i - first
    scores(first, 1)
    attend(i, 0, True)
    trips = jnp.maximum(n_full - 1, 0) // 2

    def body(tt, carry):
        j = first + 2 * tt
        scores(j + 1, 0)
        attend(j, 1, False)
        scores(j + 2, 1)
        attend(j + 1, 0, False)
        return carry

    lax.fori_loop(0, trips, body, 0)
    left = n_full - 2 * trips
    j_left = first + 2 * trips

    @pl.when(left == 1)
    def _():
        attend(j_left, 1, False)

    @pl.when(left == 2)
    def _():
        scores(j_left + 1, 0)
        attend(j_left, 1, False)
        attend(j_left + 1, 0, False)

    acc = acc_sc[...]
    if decay:
        denom = acc[:, FOX_DIM:FOX_DIM + 1]
    else:
        denom = jnp.sum(l_sc[...], axis=-1, keepdims=True)
    o_ref[0, 0] = acc / denom


def _sweep(q, k, v, cq=None, ck=None, *, chunk_mask):
    b, nh, s, _ = q.shape
    t = ATT_TILE
    decay = cq is not None
    q_spec = pl.BlockSpec((1, 1, t, LANES), lambda bi, hi, i: (bi, hi, i, 0))
    kv_spec = pl.BlockSpec((1, 1, s, LANES), lambda bi, hi, i: (bi, hi, 0, 0))
    in_specs = [q_spec, kv_spec, kv_spec]
    args = [q, k, v]
    scratch = [pltpu.VMEM((t, LANES), F32), pltpu.VMEM((t, LANES), F32),
               pltpu.VMEM((t, LANES), F32), pltpu.VMEM((2, t, t), F32)]
    if decay:
        in_specs += [pl.BlockSpec((1, 1, t, 1), lambda bi, hi, i: (bi, hi, i, 0)),
                     pl.BlockSpec((1, 1, s // t, t), lambda bi, hi, i: (bi, hi, 0, 0))]
        args += [cq, ck]
        scratch += [pltpu.VMEM((1, 1), F32), pltpu.VMEM((s // t, 1), F32)]
    return pl.pallas_call(
        functools.partial(_sweep_kernel, decay=decay, chunk_mask=chunk_mask),
        grid=(b, nh, s // t),
        in_specs=in_specs,
        out_specs=q_spec,
        out_shape=jax.ShapeDtypeStruct((b, nh, s, LANES), F32),
        scratch_shapes=scratch,
        compiler_params=_params("arbitrary", "arbitrary", "arbitrary"),
        name="fox_sweep" if decay else "mla_sweep",
    )(*args)


def _rel_bias_kernel(tab_ref, o_ref):
    qi = lax.broadcasted_iota(jnp.int32, (CHUNK, BAND), 0)
    ki = lax.broadcasted_iota(jnp.int32, (CHUNK, BAND), 1)
    idx = jnp.clip(qi + CHK_LEFT * CHUNK - ki, -(CHUNK - 1), REL_MAX) + (CHUNK - 1)
    for hd in range(CHK_HEADS):
        def body(r, acc):
            return jnp.where(idx == r, tab_ref[hd, r], acc)
        o_ref[hd] = lax.fori_loop(0, REL_SIZE, body, jnp.zeros((CHUNK, BAND), F32)) * LOG2E


def _rel_bias(table):
    return pl.pallas_call(
        _rel_bias_kernel,
        in_specs=[pl.BlockSpec(memory_space=pltpu.SMEM)],
        out_specs=pl.BlockSpec(memory_space=pltpu.VMEM),
        out_shape=jax.ShapeDtypeStruct((CHK_HEADS, CHUNK, BAND), F32),
        name="rel_bias",
    )(table)


def _chunk_kernel(q_ref, k_ref, v_ref, bias_ref, o_ref):
    i = pl.program_id(1)
    start = pl.multiple_of(i * CHK_TILE, CHK_TILE)
    q = q_ref[0]
    kw = k_ref[0, pl.ds(start, CHK_WIN), :]
    vw = v_ref[0, pl.ds(start, CHK_WIN), :]
    col = _lane_iota((CHK_TILE, CHK_WIN))
    valid = col >= CHK_LEFT * CHUNK - start
    heads = [slice(hd * CHK_DIM, (hd + 1) * CHK_DIM) for hd in range(CHK_HEADS)]
    scores = [lax.dot_general(q[:, sl], kw[:, sl], NT_DIMS, preferred_element_type=F32)
              for sl in heads]
    outs = []
    for hd, sl in enumerate(heads):
        s = jnp.where(valid, scores[hd] + bias_ref[hd], NEG)
        m = jnp.max(s, axis=-1, keepdims=True)
        p = jnp.exp2(s - m)
        denom = jnp.sum(p, axis=-1, keepdims=True)
        outs.append(jnp.dot(p.astype(BF16), vw[:, sl], preferred_element_type=F32) / denom)
    o_ref[0] = jnp.concatenate(outs, axis=-1)


def _chunk_attn(q, kp, vp, bias):
    b, s, _ = q.shape
    sp = kp.shape[1]
    return pl.pallas_call(
        _chunk_kernel,
        grid=(b, s // CHK_TILE),
        in_specs=[pl.BlockSpec((1, CHK_TILE, C_WIDTH), lambda bi, i: (bi, i, 0)),
                  pl.BlockSpec((1, sp, C_WIDTH), lambda bi, i: (bi, 0, 0)),
                  pl.BlockSpec((1, sp, C_WIDTH), lambda bi, i: (bi, 0, 0)),
                  _const_spec((CHK_HEADS, CHK_TILE, CHK_WIN))],
        out_specs=pl.BlockSpec((1, CHK_TILE, C_WIDTH), lambda bi, i: (bi, i, 0)),
        out_shape=jax.ShapeDtypeStruct((b, s, C_WIDTH), F32),
        compiler_params=_params("arbitrary", "arbitrary"),
        name="chunk_attn",
    )(q, kp, vp, bias)


def _mem_kernel(m_ref, g_ref, w_ref, k_ref, v_ref):
    h = _rms(m_ref[0], g_ref[...], D_MODEL).astype(BF16)
    kv = jnp.dot(h, w_ref[...], preferred_element_type=F32)
    k_ref[0] = kv[:, :CROSS_WIDTH].astype(BF16)
    v_ref[0] = kv[:, CROSS_WIDTH:].astype(BF16)


def _mem_kv(mem, g, w):
    b = mem.shape[0]
    spec = pl.BlockSpec((1, MEM_LEN, CROSS_WIDTH), lambda bi: (bi, 0, 0))
    shape = jax.ShapeDtypeStruct((b, MEM_LEN, CROSS_WIDTH), BF16)
    return pl.pallas_call(
        _mem_kernel,
        grid=(b,),
        in_specs=[pl.BlockSpec((1, MEM_LEN, D_MODEL), lambda bi: (bi, 0, 0)),
                  _const_spec((1, D_MODEL)), _const_spec((D_MODEL, 2 * CROSS_WIDTH))],
        out_specs=[spec, spec],
        out_shape=[shape, shape],
        compiler_params=_params("arbitrary"),
        name="mem_kv",
    )(mem, g, w)


def _mix_cross_kernel(x_ref, ya_ref, yb_ref, yc_ref, ga_ref, gb_ref, gc_ref, woa_ref, wob_ref,
                      woc_ref, gx_ref, wcq_ref, km_ref, vm_ref, wco_ref, o_ref):
    tm = x_ref.shape[1]
    lane = _lane_iota((tm, LANES))
    ya = jnp.concatenate([ya_ref[0, hd] for hd in range(MLA_HEADS)], axis=-1)
    yb = jnp.concatenate([jnp.where(lane < FOX_DIM, yb_ref[0, hd], 0.0)
                          for hd in range(FOX_HEADS)], axis=-1)
    x1 = (x_ref[0]
          + jnp.dot(_rms(ya, ga_ref[...], A_WIDTH).astype(BF16), woa_ref[...],
                    preferred_element_type=F32)
          + jnp.dot(_rms(yb, gb_ref[...], B_WIDTH).astype(BF16), wob_ref[...],
                    preferred_element_type=F32)
          + jnp.dot(_rms(yc_ref[0], gc_ref[...], C_WIDTH).astype(BF16), woc_ref[...],
                    preferred_element_type=F32))
    h = _rms(x1, gx_ref[...], D_MODEL).astype(BF16)
    q = (jnp.dot(h, wcq_ref[...], preferred_element_type=F32)
         * (CROSS_DIM ** -0.5 * LOG2E)).astype(BF16)
    outs = []
    for hd in range(CROSS_HEADS):
        sl = slice(hd * CROSS_DIM, (hd + 1) * CROSS_DIM)
        s = lax.dot_general(q[:, sl], km_ref[0, :, sl], NT_DIMS, preferred_element_type=F32)
        m = jnp.max(s, axis=-1, keepdims=True)
        p = jnp.exp2(s - m)
        denom = jnp.sum(p, axis=-1, keepdims=True)
        outs.append(jnp.dot(p.astype(BF16), vm_ref[0, :, sl], preferred_element_type=F32)
                    / denom)
    o = jnp.concatenate(outs, axis=-1).astype(BF16)
    o_ref[0] = x1 + jnp.dot(o, wco_ref[...], preferred_element_type=F32)


def _mix_cross(x, ya, yb, yc, ga, gb, gc, woa, wob, woc, gx, wcq, km, vm, wco):
    b, s, _ = x.shape
    tm = TOK_TILE
    x_spec = pl.BlockSpec((1, tm, D_MODEL), lambda bi, i: (bi, i, 0))
    head_spec = pl.BlockSpec((1, 4, tm, LANES), lambda bi, i: (bi, 0, i, 0))
    mem_spec = pl.BlockSpec((1, MEM_LEN, CROSS_WIDTH), lambda bi, i: (bi, 0, 0))
    return pl.pallas_call(
        _mix_cross_kernel,
        grid=(b, s // tm),
        in_specs=[x_spec, head_spec, head_spec,
                  pl.BlockSpec((1, tm, C_WIDTH), lambda bi, i: (bi, i, 0)),
                  _const_spec((1, A_WIDTH)), _const_spec((1, 512)), _const_spec((1, C_WIDTH)),
                  _const_spec((A_WIDTH, D_MODEL)), _const_spec((512, D_MODEL)),
                  _const_spec((C_WIDTH, D_MODEL)), _const_spec((1, D_MODEL)),
                  _const_spec((D_MODEL, CROSS_WIDTH)), mem_spec, mem_spec,
                  _const_spec((CROSS_WIDTH, D_MODEL))],
        out_specs=x_spec,
        out_shape=jax.ShapeDtypeStruct(x.shape, F32),
        compiler_params=_params("arbitrary", "arbitrary"),
        name="mix_cross",
    )(x, ya, yb, yc, ga, gb, gc, woa, wob, woc, gx, wcq, km, vm, wco)


def _ffn_kernel(x_ref, g_ref, wg_ref, wu_ref, wd_ref, gf_ref, o_ref, acc_ref, *, final_norm):
    x = x_ref[0]
    h = _rms(x, g_ref[...], D_MODEL).astype(BF16)
    acc_ref[...] = x
    nblk = FFN_HIDDEN // FFN_BLOCK

    def gate_up(c):
        gate = jnp.dot(h, wg_ref[c], preferred_element_type=F32)
        up = jnp.dot(h, wu_ref[c], preferred_element_type=F32)
        return gate, up

    nxt = gate_up(0)
    for c in range(nblk):
        gate, up = nxt
        if c + 1 < nblk:
            nxt = gate_up(c + 1)
        act = (gate * jax.nn.sigmoid(gate) * up).astype(BF16)
        acc_ref[...] += jnp.dot(act, wd_ref[c], preferred_element_type=F32)
    y = acc_ref[...]
    o_ref[0] = _rms(y, gf_ref[...], D_MODEL) if final_norm else y


def _ffn(x, g, wg, wu, wd, gf, *, final_norm):
    b, s, _ = x.shape
    tm = TOK_TILE
    nblk = FFN_HIDDEN // FFN_BLOCK
    x_spec = pl.BlockSpec((1, tm, D_MODEL), lambda bi, i: (bi, i, 0))

    def resident(shape):
        return pl.BlockSpec(shape, lambda bi, i: (0, 0, 0), pipeline_mode=pl.Buffered(1))

    return pl.pallas_call(
        functools.partial(_ffn_kernel, final_norm=final_norm),
        grid=(b, s // tm),
        in_specs=[x_spec, _const_spec((1, D_MODEL)),
                  resident((nblk, D_MODEL, FFN_BLOCK)), resident((nblk, D_MODEL, FFN_BLOCK)),
                  resident((nblk, FFN_BLOCK, D_MODEL)), _const_spec((1, D_MODEL))],
        out_specs=x_spec,
        out_shape=jax.ShapeDtypeStruct(x.shape, F32),
        scratch_shapes=[pltpu.VMEM((tm, D_MODEL), F32)],
        compiler_params=_params("arbitrary", "arbitrary"),
        name="ffn",
    )(x, g, wg, wu, wd, gf)


def _pad_cols(a, width):
    return jnp.pad(a, ((0, 0), (0, width - a.shape[1])))


def _layer_weights(w_in, w_uq, w_ukv, w_o, out_norm, w_gu, w_down, f_bias):
    sizes = (256, 128, 32, 256, 256, 256, 4, 256, 256, 256)
    parts, at = [], 0
    for n in sizes:
        parts.append(w_in[:, at:at + n])
        at += n
    c_q, c_kv, k_rope, fq, fk, fv, f_logit, cq, ck, cv = parts
    zeros = lambda n: jnp.zeros((D_MODEL, n), F32)
    misc = jnp.concatenate([f_logit, zeros(MLA_NOPE - FOX_HEADS), k_rope, zeros(32)], axis=1)
    w_in_r = jnp.concatenate([c_q, c_kv, misc, fq, fk, fv, cq, ck, cv], axis=1).astype(BF16)
    wuq = jnp.pad(w_uq.reshape(MLA_Q_RANK, MLA_HEADS, MLA_NOPE + MLA_ROPE),
                  ((0, 0), (0, 0), (0, 32))).reshape(MLA_Q_RANK, 512).astype(BF16)
    wukv = w_ukv.reshape(MLA_KV_RANK, MLA_HEADS, MLA_NOPE + MLA_V)
    wk = jnp.pad(wukv[:, :, :MLA_NOPE], ((0, 0), (0, 0), (0, 64))).reshape(MLA_KV_RANK, 512)
    wv = wukv[:, :, MLA_NOPE:].reshape(MLA_KV_RANK, 512)
    wukv_r = jnp.concatenate([wk, wv], axis=1).astype(BF16)
    woa = w_o[:A_WIDTH].astype(BF16)
    wob = jnp.pad(w_o[A_WIDTH:A_WIDTH + B_WIDTH].reshape(FOX_HEADS, FOX_DIM, D_MODEL),
                  ((0, 0), (0, 64), (0, 0))).reshape(512, D_MODEL).astype(BF16)
    woc = w_o[A_WIDTH + B_WIDTH:].astype(BF16)
    ga = out_norm[:A_WIDTH].reshape(1, A_WIDTH)
    gb = jnp.pad(out_norm[A_WIDTH:A_WIDTH + B_WIDTH].reshape(FOX_HEADS, FOX_DIM),
                 ((0, 0), (0, 64))).reshape(1, 512)
    gc = out_norm[A_WIDTH + B_WIDTH:].reshape(1, C_WIDTH)
    nblk = FFN_HIDDEN // FFN_BLOCK
    wg = w_gu[:, :FFN_HIDDEN].reshape(D_MODEL, nblk, FFN_BLOCK).transpose(1, 0, 2).astype(BF16)
    wu = w_gu[:, FFN_HIDDEN:].reshape(D_MODEL, nblk, FFN_BLOCK).transpose(1, 0, 2).astype(BF16)
    wd = w_down.reshape(nblk, FFN_BLOCK, D_MODEL).astype(BF16)
    fb = _pad_cols(f_bias.reshape(1, FOX_HEADS), LANES)
    return w_in_r, wuq, wukv_r, woa, wob, woc, ga, gb, gc, wg, wu, wd, fb


def _rope_tables(seq):
    pos = jnp.arange(seq, dtype=F32)
    inv = ROPE_THETA ** (-jnp.arange(0, MLA_ROPE, 2, dtype=F32) / MLA_ROPE)
    ang = pos[:, None] * inv[None, :]
    cos, sin = jnp.cos(ang), jnp.sin(ang)
    z = lambda n: jnp.zeros((seq, n), F32)
    half = MLA_ROPE // 2
    tcos = jnp.concatenate([z(MLA_NOPE), cos, cos, z(32)], axis=1)
    ts1 = jnp.concatenate([z(MLA_NOPE), -sin, z(half), z(32)], axis=1)
    ts2 = jnp.concatenate([z(MLA_NOPE), z(half), sin, z(32)], axis=1)
    return tcos, ts1, ts2


def _tile_bias(bias):
    nq = CHK_TILE // CHUNK
    rows = [jnp.pad(bias, ((0, 0), (0, 0), (c * CHUNK, CHK_WIN - BAND - c * CHUNK)),
                    constant_values=NEG) for c in range(nq)]
    return jnp.concatenate(rows, axis=1)


def kernel(x, mem, norm_mix, w_in, q_norm, w_uq, kv_norm, w_ukv, f_bias, rel_bias, out_norm, w_o,
           norm_cross, norm_mem, w_cq, w_ckv, w_co, norm_ffn, w_gu, w_down, final_norm):
    b, s, _ = x.shape
    depth = w_in.shape[0]
    tcos, ts1, ts2 = _rope_tables(s)
    tri = jnp.tril(jnp.ones((TOK_TILE, TOK_TILE), BF16))
    row = lambda v: v.reshape(1, -1)
    for l in range(depth):
        (w_in_r, wuq, wukv_r, woa, wob, woc, ga, gb, gc, wg, wu, wd, fb) = _layer_weights(
            w_in[l], w_uq[l], w_ukv[l], w_o[l], out_norm[l], w_gu[l], w_down[l], f_bias[l])
        mq, mk, mv, fq, fk, fv, cq, ck, cv, cum = _in_proj(
            x, row(norm_mix[l]), w_in_r, row(q_norm[l]), wuq, row(kv_norm[l]), wukv_r,
            tcos, ts1, ts2, fb, tri)
        ya = _sweep(mq, mk, mv, chunk_mask=True)
        cum_h = jnp.transpose(cum[:, :, :FOX_HEADS], (0, 2, 1))
        yb = _sweep(fq, fk, fv, cum_h.reshape(b, FOX_HEADS, s, 1),
                    cum_h.reshape(b, FOX_HEADS, s // ATT_TILE, ATT_TILE), chunk_mask=False)
        pad = ((0, 0), (CHK_LEFT * CHUNK, 0), (0, 0))
        yc = _chunk_attn(cq, jnp.pad(ck, pad), jnp.pad(cv, pad), _tile_bias(_rel_bias(rel_bias[l])))
        km, vm = _mem_kv(mem, row(norm_mem[l]), w_ckv[l].astype(BF16))
        x = _mix_cross(x, ya, yb, yc, ga, gb, gc, woa, wob, woc, row(norm_cross[l]),
                       w_cq[l].astype(BF16), km, vm, w_co[l].astype(BF16))
        x = _ffn(x, row(norm_ffn[l]), wg, wu, wd, row(final_norm), final_norm=(l == depth - 1))
    return x
```

```python
import functools
import math

import jax
import jax.numpy as jnp
from jax import lax
from jax.experimental import pallas as pl
from jax.experimental.pallas import tpu as pltpu

D_MODEL = 1024
CHUNK = 64
MEM_LEN = 256
EPS = 1e-6
MLA_HEADS = 4
MLA_Q_RANK = 256
MLA_KV_RANK = 128
MLA_NOPE = 64
MLA_ROPE = 32
MLA_V = 128
ROPE_THETA = 10000.0
FOX_HEADS = 4
FOX_DIM = 64
CHK_HEADS = 4
CHK_DIM = 64
CHK_LEFT = 8
BAND = (CHK_LEFT + 1) * CHUNK
REL_MAX = 128
REL_SIZE = (CHUNK - 1) + REL_MAX + 1
A_WIDTH = MLA_HEADS * MLA_V
B_WIDTH = FOX_HEADS * FOX_DIM
C_WIDTH = CHK_HEADS * CHK_DIM
CROSS_HEADS = 4
CROSS_DIM = 128
CROSS_WIDTH = CROSS_HEADS * CROSS_DIM
FFN_HIDDEN = 2816

LANES = 128
LOG2E = 1.4426950408889634
NEG = -1e30
SKIP_LOG2 = 152.0
VMEM_LIMIT = 56 * 1024 * 1024

TOK_TILE = 512
ATT_TILE = 512
CHK_TILE = 256
CHK_WIN = CHK_TILE + CHK_LEFT * CHUNK
FFN_BLOCK = 256
IN_COLS = 2048

BF16 = jnp.bfloat16
F32 = jnp.float32
NT_DIMS = (((1,), (1,)), ((), ()))


def _params(*semantics):
    return pltpu.CompilerParams(dimension_semantics=semantics, vmem_limit_bytes=VMEM_LIMIT)


def _rms(xf, gain, width):
    return xf * lax.rsqrt(jnp.sum(xf * xf, axis=-1, keepdims=True) * (1.0 / width) + EPS) * gain


def _lane_iota(shape):
    return lax.broadcasted_iota(jnp.int32, shape, len(shape) - 1)


def _const_spec(shape):
    zeros = (0,) * len(shape)
    return pl.BlockSpec(shape, lambda *_: zeros)


def _in_proj_kernel(x_ref, g_ref, w_ref, qn_ref, wuq_ref, kvn_ref, wukv_ref, tc_ref, ts1_ref,
                    ts2_ref, fb_ref, tri_ref,
                    mq_ref, mk_ref, mv_ref, fq_ref, fk_ref, fv_ref, cq_ref, ck_ref, cv_ref,
                    cum_ref, carry_ref):
    i = pl.program_id(1)
    tm = x_ref.shape[1]
    h = _rms(x_ref[0], g_ref[...], D_MODEL).astype(BF16)

    def proj(lo, hi):
        return jnp.dot(h, w_ref[:, lo:hi], preferred_element_type=F32)

    lane = _lane_iota((tm, LANES))
    tcos, ts1, ts2 = tc_ref[...], ts1_ref[...], ts2_ref[...]

    def rope(y, keep_low):
        base = tcos + jnp.where(lane < MLA_NOPE, 1.0, 0.0) if keep_low else tcos
        return (y * base + pltpu.roll(y, LANES - MLA_ROPE // 2, 1) * ts1
                + pltpu.roll(y, MLA_ROPE // 2, 1) * ts2)

    c_q = proj(0, 256)
    q_up = jnp.dot(_rms(c_q, qn_ref[...], MLA_Q_RANK).astype(BF16), wuq_ref[...],
                   preferred_element_type=F32)
    misc = proj(384, 512)
    k_pe = rope(misc, False)
    c_kv = proj(256, 384)
    kv_up = jnp.dot(_rms(c_kv, kvn_ref[...], MLA_KV_RANK).astype(BF16), wukv_ref[...],
                    preferred_element_type=F32)
    q_scale = (MLA_NOPE + MLA_ROPE) ** -0.5 * LOG2E
    for hd in range(MLA_HEADS):
        sl = slice(hd * LANES, (hd + 1) * LANES)
        mq_ref[0, hd] = (rope(q_up[:, sl], True) * q_scale).astype(BF16)
        mk_ref[0, hd] = (kv_up[:, sl] + k_pe).astype(BF16)
        mv_ref[0, hd] = kv_up[:, A_WIDTH + hd * LANES:A_WIDTH + (hd + 1) * LANES].astype(BF16)

    def head_block(y, hd):
        blk = y[:, (hd // 2) * LANES:(hd // 2 + 1) * LANES]
        return pltpu.roll(blk, FOX_DIM, 1) if hd % 2 else blk

    fq = proj(512, 768)
    fk = proj(768, 1024)
    fv = proj(1024, 1280)
    f_scale = FOX_DIM ** -0.5 * LOG2E
    ones_col = jnp.where(lane == FOX_DIM, 1.0, 0.0)
    for hd in range(FOX_HEADS):
        fq_ref[0, hd] = jnp.where(lane < FOX_DIM, head_block(fq, hd) * f_scale, 0.0).astype(BF16)
        fk_ref[0, hd] = jnp.where(lane < FOX_DIM, head_block(fk, hd), 0.0).astype(BF16)
        fv_ref[0, hd] = jnp.where(lane < FOX_DIM, head_block(fv, hd), ones_col).astype(BF16)

    z = misc + fb_ref[...]
    log_f = jnp.minimum(z, 0.0) - jnp.log1p(jnp.exp(-jnp.abs(z)))
    log_f = jnp.where(lane < FOX_HEADS, log_f, 0.0)
    p_hi = log_f.astype(BF16)
    rem = log_f - p_hi.astype(F32)
    p_mid = rem.astype(BF16)
    p_lo = (rem - p_mid.astype(F32)).astype(BF16)
    parts = jnp.dot(tri_ref[...], jnp.concatenate([p_hi, p_mid, p_lo], axis=-1),
                    preferred_element_type=F32)
    cum = parts[:, :LANES] + parts[:, LANES:2 * LANES] + parts[:, 2 * LANES:]

    @pl.when(i == 0)
    def _():
        carry_ref[...] = jnp.zeros_like(carry_ref)

    cum = cum + carry_ref[...]
    carry_ref[...] = cum[tm - 1:tm, :]
    cum_ref[0] = cum * LOG2E

    cq_ref[0] = (proj(1280, 1536) * (CHK_DIM ** -0.5 * LOG2E)).astype(BF16)
    ck_ref[0] = proj(1536, 1792).astype(BF16)
    cv_ref[0] = proj(1792, 2048).astype(BF16)


def _in_proj(x, g, w, qn, wuq, kvn, wukv, tcos, ts1, ts2, fb, tri):
    b, s, _ = x.shape
    tm = TOK_TILE
    head_spec = pl.BlockSpec((1, 4, tm, LANES), lambda bi, i: (bi, 0, i, 0))
    dense_spec = pl.BlockSpec((1, tm, 256), lambda bi, i: (bi, i, 0))
    tab_spec = pl.BlockSpec((tm, LANES), lambda bi, i: (i, 0))
    head_shape = jax.ShapeDtypeStruct((b, 4, s, LANES), BF16)
    dense_shape = jax.ShapeDtypeStruct((b, s, 256), BF16)
    return pl.pallas_call(
        _in_proj_kernel,
        grid=(b, s // tm),
        in_specs=[pl.BlockSpec((1, tm, D_MODEL), lambda bi, i: (bi, i, 0)),
                  _const_spec((1, D_MODEL)), _const_spec((D_MODEL, IN_COLS)),
                  _const_spec((1, MLA_Q_RANK)), _const_spec((MLA_Q_RANK, 512)),
                  _const_spec((1, MLA_KV_RANK)), _const_spec((MLA_KV_RANK, 1024)),
                  tab_spec, tab_spec, tab_spec, _const_spec((1, LANES)), _const_spec((tm, tm))],
        out_specs=[head_spec] * 6 + [dense_spec] * 3
                  + [pl.BlockSpec((1, tm, LANES), lambda bi, i: (bi, i, 0))],
        out_shape=[head_shape] * 6 + [dense_shape] * 3
                  + [jax.ShapeDtypeStruct((b, s, LANES), F32)],
        scratch_shapes=[pltpu.VMEM((1, LANES), F32)],
        compiler_params=_params("arbitrary", "arbitrary"),
        name="in_proj",
    )(x, g, w, qn, wuq, kvn, wukv, tcos, ts1, ts2, fb, tri)


def _sweep_kernel(*refs, decay, chunk_mask):
    if decay:
        (q_ref, k_ref, v_ref, cq_ref, ck_ref, o_ref,
         m_sc, l_sc, acc_sc, s_sc, p_sc, alpha_sc, kmax_sc, ckmin_sc) = refs
    else:
        q_ref, k_ref, v_ref, o_ref, m_sc, l_sc, acc_sc, s_sc, p_sc, alpha_sc = refs
    i = pl.program_id(2)
    t = ATT_TILE
    q = q_ref[0, 0]
    m_sc[...] = jnp.full_like(m_sc, NEG)
    l_sc[...] = jnp.zeros_like(l_sc)
    acc_sc[...] = jnp.zeros_like(acc_sc)
    if decay:
        nblk = ck_ref.shape[2]

        @pl.when(i == 0)
        def _():
            def key_norm(j, acc):
                kf = k_ref[0, 0, pl.ds(pl.multiple_of(j * t, t), t), :].astype(F32)
                return jnp.maximum(acc, jnp.max(jnp.sum(kf * kf, axis=-1, keepdims=True),
                                                axis=0, keepdims=True))
            kmax_sc[...] = jnp.sqrt(lax.fori_loop(0, nblk, key_norm, jnp.zeros((1, 1), F32)))
            ckmin_sc[...] = jnp.min(ck_ref[0, 0], axis=-1, keepdims=True)

        cq = cq_ref[0, 0]
        cq_rep = jnp.broadcast_to(cq, (t, LANES))

    def first_needed_block():
        if not decay:
            return 0
        qf = q.astype(F32)
        kd = k_ref[0, 0, pl.ds(pl.multiple_of(i * t, t), t), :].astype(F32)
        gap = (jnp.sqrt(jnp.sum(qf * qf, axis=-1, keepdims=True)) * kmax_sc[...]
               - jnp.sum(qf * kd, axis=-1, keepdims=True) + cq)
        gap_max = jnp.max(gap, axis=0, keepdims=True)
        needed = gap_max - ckmin_sc[...] >= -SKIP_LOG2
        blk = lax.broadcasted_iota(jnp.int32, (nblk, 1), 0)
        return jnp.min(jnp.where(needed, blk, i))

    def scores(j, slot):
        off = pl.multiple_of(j * t, t)
        k = k_ref[0, 0, pl.ds(off, t), :]
        s_sc[slot] = lax.dot_general(q, k, NT_DIMS, preferred_element_type=F32)

    def softmax(j, slot, masked):
        s = s_sc[slot]
        if decay:
            s = s - ck_ref[0, 0, pl.ds(j, 1), :]
        if masked:
            qi = lax.broadcasted_iota(jnp.int32, (t, t), 0)
            ki = lax.broadcasted_iota(jnp.int32, (t, t), 1)
            if chunk_mask:
                qi, ki = qi // CHUNK, ki // CHUNK
            s = jnp.where(ki <= qi, s, NEG)
        m_old = m_sc[...]
        m_cur = jnp.max(s, axis=-1, keepdims=True)
        if decay:
            m_cur = m_cur + cq_rep
        m_new = jnp.maximum(m_old, m_cur)
        alpha = jnp.exp2(m_old - m_new)
        shift = m_new - cq_rep if decay else m_new
        ps = [jnp.exp2(s[:, c * LANES:(c + 1) * LANES] - shift) for c in range(t // LANES)]
        if not decay:
            l_sc[...] = alpha * l_sc[...] + functools.reduce(lambda a, b: a + b, ps)
        p_sc[slot] = jnp.concatenate(ps, axis=-1).astype(BF16)
        alpha_sc[slot] = alpha
        m_sc[...] = m_new

    def weigh(j, slot):
        off = pl.multiple_of(j * t, t)
        v = v_ref[0, 0, pl.ds(off, t), :]
        acc_sc[...] = alpha_sc[slot] * acc_sc[...] + jnp.dot(p_sc[slot], v,
                                                              preferred_element_type=F32)

    scores(i, 0)
    first = first_needed_block()
    n_full = i - first
    scores(first, 1)
    softmax(i, 0, True)
    trips = jnp.maximum(n_full - 1, 0) // 2

    def body(tt, carry):
        j = first + 2 * tt
        weigh(jnp.where(tt == 0, i, j - 1), 0)
        scores(j + 1, 0)
        softmax(j, 1, False)
        weigh(j, 1)
        scores(j + 2, 1)
        softmax(j + 1, 0, False)
        return carry

    lax.fori_loop(0, trips, body, 0)
    left = n_full - 2 * trips
    j_left = first + 2 * trips
    j_pending = jnp.where(trips == 0, i, j_left - 1)

    @pl.when(left == 0)
    def _():
        weigh(j_pending, 0)

    @pl.when(left == 1)
    def _():
        weigh(j_pending, 0)
        softmax(j_left, 1, False)
        weigh(j_left, 1)

    @pl.when(left == 2)
    def _():
        weigh(j_pending, 0)
        scores(j_left + 1, 0)
        softmax(j_left, 1, False)
        weigh(j_left, 1)
        softmax(j_left + 1, 0, False)
        weigh(j_left + 1, 0)

    acc = acc_sc[...]
    if decay:
        denom = acc[:, FOX_DIM:FOX_DIM + 1]
    else:
        denom = jnp.sum(l_sc[...], axis=-1, keepdims=True)
    o_ref[0, 0] = acc / denom


def _sweep(q, k, v, cq=None, ck=None, *, chunk_mask):
    b, nh, s, _ = q.shape
    t = ATT_TILE
    decay = cq is not None
    q_spec = pl.BlockSpec((1, 1, t, LANES), lambda bi, hi, i: (bi, hi, i, 0))
    kv_spec = pl.BlockSpec((1, 1, s, LANES), lambda bi, hi, i: (bi, hi, 0, 0))
    in_specs = [q_spec, kv_spec, kv_spec]
    args = [q, k, v]
    scratch = [pltpu.VMEM((t, LANES), F32), pltpu.VMEM((t, LANES), F32),
               pltpu.VMEM((t, LANES), F32), pltpu.VMEM((2, t, t), F32),
               pltpu.VMEM((2, t, t), BF16), pltpu.VMEM((2, t, LANES), F32)]
    if decay:
        in_specs += [pl.BlockSpec((1, 1, t, 1), lambda bi, hi, i: (bi, hi, i, 0)),
                     pl.BlockSpec((1, 1, s // t, t), lambda bi, hi, i: (bi, hi, 0, 0))]
        args += [cq, ck]
        scratch += [pltpu.VMEM((1, 1), F32), pltpu.VMEM((s // t, 1), F32)]
    return pl.pallas_call(
        functools.partial(_sweep_kernel, decay=decay, chunk_mask=chunk_mask),
        grid=(b, nh, s // t),
        in_specs=in_specs,
        out_specs=q_spec,
        out_shape=jax.ShapeDtypeStruct((b, nh, s, LANES), F32),
        scratch_shapes=scratch,
        compiler_params=_params("arbitrary", "arbitrary", "arbitrary"),
        name="fox_sweep" if decay else "mla_sweep",
    )(*args)


def _rel_bias_kernel(tab_ref, o_ref):
    qi = lax.broadcasted_iota(jnp.int32, (CHUNK, BAND), 0)
    ki = lax.broadcasted_iota(jnp.int32, (CHUNK, BAND), 1)
    idx = jnp.clip(qi + CHK_LEFT * CHUNK - ki, -(CHUNK - 1), REL_MAX) + (CHUNK - 1)
    for hd in range(CHK_HEADS):
        def body(r, acc):
            return jnp.where(idx == r, tab_ref[hd, r], acc)
        o_ref[hd] = lax.fori_loop(0, REL_SIZE, body, jnp.zeros((CHUNK, BAND), F32)) * LOG2E


def _rel_bias(table):
    return pl.pallas_call(
        _rel_bias_kernel,
        in_specs=[pl.BlockSpec(memory_space=pltpu.SMEM)],
        out_specs=pl.BlockSpec(memory_space=pltpu.VMEM),
        out_shape=jax.ShapeDtypeStruct((CHK_HEADS, CHUNK, BAND), F32),
        name="rel_bias",
    )(table)


def _chunk_kernel(q_ref, k_ref, v_ref, bias_ref, o_ref):
    i = pl.program_id(1)
    start = pl.multiple_of(i * CHK_TILE, CHK_TILE)
    q = q_ref[0]
    kw = k_ref[0, pl.ds(start, CHK_WIN), :]
    vw = v_ref[0, pl.ds(start, CHK_WIN), :]
    col = _lane_iota((CHK_TILE, CHK_WIN))
    valid = col >= CHK_LEFT * CHUNK - start
    heads = [slice(hd * CHK_DIM, (hd + 1) * CHK_DIM) for hd in range(CHK_HEADS)]
    scores = [lax.dot_general(q[:, sl], kw[:, sl], NT_DIMS, preferred_element_type=F32)
              for sl in heads]
    outs = []
    for hd, sl in enumerate(heads):
        s = jnp.where(valid, scores[hd] + bias_ref[hd], NEG)
        m = jnp.max(s, axis=-1, keepdims=True)
        p = jnp.exp2(s - m)
        denom = jnp.sum(p, axis=-1, keepdims=True)
        outs.append(jnp.dot(p.astype(BF16), vw[:, sl], preferred_element_type=F32) / denom)
    o_ref[0] = jnp.concatenate(outs, axis=-1)


def _chunk_attn(q, kp, vp, bias):
    b, s, _ = q.shape
    sp = kp.shape[1]
    return pl.pallas_call(
        _chunk_kernel,
        grid=(b, s // CHK_TILE),
        in_specs=[pl.BlockSpec((1, CHK_TILE, C_WIDTH), lambda bi, i: (bi, i, 0)),
                  pl.BlockSpec((1, sp, C_WIDTH), lambda bi, i: (bi, 0, 0)),
                  pl.BlockSpec((1, sp, C_WIDTH), lambda bi, i: (bi, 0, 0)),
                  _const_spec((CHK_HEADS, CHK_TILE, CHK_WIN))],
        out_specs=pl.BlockSpec((1, CHK_TILE, C_WIDTH), lambda bi, i: (bi, i, 0)),
        out_shape=jax.ShapeDtypeStruct((b, s, C_WIDTH), F32),
        compiler_params=_params("arbitrary", "arbitrary"),
        name="chunk_attn",
    )(q, kp, vp, bias)


def _mem_kernel(m_ref, g_ref, w_ref, k_ref, v_ref):
    h = _rms(m_ref[0], g_ref[...], D_MODEL).astype(BF16)
    kv = jnp.dot(h, w_ref[...], preferred_element_type=F32)
    k_ref[0] = kv[:, :CROSS_WIDTH].astype(BF16)
    v_ref[0] = kv[:, CROSS_WIDTH:].astype(BF16)


def _mem_kv(mem, g, w):
    b = mem.shape[0]
    spec = pl.BlockSpec((1, MEM_LEN, CROSS_WIDTH), lambda bi: (bi, 0, 0))
    shape = jax.ShapeDtypeStruct((b, MEM_LEN, CROSS_WIDTH), BF16)
    return pl.pallas_call(
        _mem_kernel,
        grid=(b,),
        in_specs=[pl.BlockSpec((1, MEM_LEN, D_MODEL), lambda bi: (bi, 0, 0)),
                  _const_spec((1, D_MODEL)), _const_spec((D_MODEL, 2 * CROSS_WIDTH))],
        out_specs=[spec, spec],
        out_shape=[shape, shape],
        compiler_params=_params("arbitrary"),
        name="mem_kv",
    )(mem, g, w)


def _mix_cross_kernel(x_ref, ya_ref, yb_ref, yc_ref, ga_ref, gb_ref, gc_ref, woa_ref, wob_ref,
                      woc_ref, gx_ref, wcq_ref, km_ref, vm_ref, wco_ref, o_ref):
    tm = x_ref.shape[1]
    lane = _lane_iota((tm, LANES))
    ya = jnp.concatenate([ya_ref[0, hd] for hd in range(MLA_HEADS)], axis=-1)
    yb = jnp.concatenate([jnp.where(lane < FOX_DIM, yb_ref[0, hd], 0.0)
                          for hd in range(FOX_HEADS)], axis=-1)
    x1 = (x_ref[0]
          + jnp.dot(_rms(ya, ga_ref[...], A_WIDTH).astype(BF16), woa_ref[...],
                    preferred_element_type=F32)
          + jnp.dot(_rms(yb, gb_ref[...], B_WIDTH).astype(BF16), wob_ref[...],
                    preferred_element_type=F32)
          + jnp.dot(_rms(yc_ref[0], gc_ref[...], C_WIDTH).astype(BF16), woc_ref[...],
                    preferred_element_type=F32))
    h = _rms(x1, gx_ref[...], D_MODEL).astype(BF16)
    q = (jnp.dot(h, wcq_ref[...], preferred_element_type=F32)
         * (CROSS_DIM ** -0.5 * LOG2E)).astype(BF16)
    outs = []
    for hd in range(CROSS_HEADS):
        sl = slice(hd * CROSS_DIM, (hd + 1) * CROSS_DIM)
        s = lax.dot_general(q[:, sl], km_ref[0, :, sl], NT_DIMS, preferred_element_type=F32)
        m = jnp.max(s, axis=-1, keepdims=True)
        p = jnp.exp2(s - m)
        denom = jnp.sum(p, axis=-1, keepdims=True)
        outs.append(jnp.dot(p.astype(BF16), vm_ref[0, :, sl], preferred_element_type=F32)
                    / denom)
    o = jnp.concatenate(outs, axis=-1).astype(BF16)
    o_ref[0] = x1 + jnp.dot(o, wco_ref[...], preferred_element_type=F32)


def _mix_cross(x, ya, yb, yc, ga, gb, gc, woa, wob, woc, gx, wcq, km, vm, wco):
    b, s, _ = x.shape
    tm = TOK_TILE
    x_spec = pl.BlockSpec((1, tm, D_MODEL), lambda bi, i: (bi, i, 0))
    head_spec = pl.BlockSpec((1, 4, tm, LANES), lambda bi, i: (bi, 0, i, 0))
    mem_spec = pl.BlockSpec((1, MEM_LEN, CROSS_WIDTH), lambda bi, i: (bi, 0, 0))
    return pl.pallas_call(
        _mix_cross_kernel,
        grid=(b, s // tm),
        in_specs=[x_spec, head_spec, head_spec,
                  pl.BlockSpec((1, tm, C_WIDTH), lambda bi, i: (bi, i, 0)),
                  _const_spec((1, A_WIDTH)), _const_spec((1, 512)), _const_spec((1, C_WIDTH)),
                  _const_spec((A_WIDTH, D_MODEL)), _const_spec((512, D_MODEL)),
                  _const_spec((C_WIDTH, D_MODEL)), _const_spec((1, D_MODEL)),
                  _const_spec((D_MODEL, CROSS_WIDTH)), mem_spec, mem_spec,
                  _const_spec((CROSS_WIDTH, D_MODEL))],
        out_specs=x_spec,
        out_shape=jax.ShapeDtypeStruct(x.shape, F32),
        compiler_params=_params("arbitrary", "arbitrary"),
        name="mix_cross",
    )(x, ya, yb, yc, ga, gb, gc, woa, wob, woc, gx, wcq, km, vm, wco)


def _ffn_kernel(x_ref, g_ref, wg_ref, wu_ref, wd_ref, gf_ref, o_ref, acc_ref, *, final_norm):
    x = x_ref[0]
    h = _rms(x, g_ref[...], D_MODEL).astype(BF16)
    acc_ref[...] = x
    nblk = FFN_HIDDEN // FFN_BLOCK

    def gate_up(c):
        gate = jnp.dot(h, wg_ref[c], preferred_element_type=F32)
        up = jnp.dot(h, wu_ref[c], preferred_element_type=F32)
        return gate, up

    nxt = gate_up(0)
    for c in range(nblk):
        gate, up = nxt
        if c + 1 < nblk:
            nxt = gate_up(c + 1)
        act = (gate * jax.nn.sigmoid(gate) * up).astype(BF16)
        acc_ref[...] += jnp.dot(act, wd_ref[c], preferred_element_type=F32)
    y = acc_ref[...]
    o_ref[0] = _rms(y, gf_ref[...], D_MODEL) if final_norm else y


def _ffn(x, g, wg, wu, wd, gf, *, final_norm):
    b, s, _ = x.shape
    tm = TOK_TILE
    nblk = FFN_HIDDEN // FFN_BLOCK
    x_spec = pl.BlockSpec((1, tm, D_MODEL), lambda bi, i: (bi, i, 0))

    def resident(shape):
        return pl.BlockSpec(shape, lambda bi, i: (0, 0, 0), pipeline_mode=pl.Buffered(1))

    return pl.pallas_call(
        functools.partial(_ffn_kernel, final_norm=final_norm),
        grid=(b, s // tm),
        in_specs=[x_spec, _const_spec((1, D_MODEL)),
                  resident((nblk, D_MODEL, FFN_BLOCK)), resident((nblk, D_MODEL, FFN_BLOCK)),
                  resident((nblk, FFN_BLOCK, D_MODEL)), _const_spec((1, D_MODEL))],
        out_specs=x_spec,
        out_shape=jax.ShapeDtypeStruct(x.shape, F32),
        scratch_shapes=[pltpu.VMEM((tm, D_MODEL), F32)],
        compiler_params=_params("arbitrary", "arbitrary"),
        name="ffn",
    )(x, g, wg, wu, wd, gf)


def _pad_cols(a, width):
    return jnp.pad(a, ((0, 0), (0, width - a.shape[1])))


def _layer_weights(w_in, w_uq, w_ukv, w_o, out_norm, w_gu, w_down, f_bias):
    sizes = (256, 128, 32, 256, 256, 256, 4, 256, 256, 256)
    parts, at = [], 0
    for n in sizes:
        parts.append(w_in[:, at:at + n])
        at += n
    c_q, c_kv, k_rope, fq, fk, fv, f_logit, cq, ck, cv = parts
    zeros = lambda n: jnp.zeros((D_MODEL, n), F32)
    misc = jnp.concatenate([f_logit, zeros(MLA_NOPE - FOX_HEADS), k_rope, zeros(32)], axis=1)
    w_in_r = jnp.concatenate([c_q, c_kv, misc, fq, fk, fv, cq, ck, cv], axis=1).astype(BF16)
    wuq = jnp.pad(w_uq.reshape(MLA_Q_RANK, MLA_HEADS, MLA_NOPE + MLA_ROPE),
                  ((0, 0), (0, 0), (0, 32))).reshape(MLA_Q_RANK, 512).astype(BF16)
    wukv = w_ukv.reshape(MLA_KV_RANK, MLA_HEADS, MLA_NOPE + MLA_V)
    wk = jnp.pad(wukv[:, :, :MLA_NOPE], ((0, 0), (0, 0), (0, 64))).reshape(MLA_KV_RANK, 512)
    wv = wukv[:, :, MLA_NOPE:].reshape(MLA_KV_RANK, 512)
    wukv_r = jnp.concatenate([wk, wv], axis=1).astype(BF16)
    woa = w_o[:A_WIDTH].astype(BF16)
    wob = jnp.pad(w_o[A_WIDTH:A_WIDTH + B_WIDTH].reshape(FOX_HEADS, FOX_DIM, D_MODEL),
                  ((0, 0), (0, 64), (0, 0))).reshape(512, D_MODEL).astype(BF16)
    woc = w_o[A_WIDTH + B_WIDTH:].astype(BF16)
    ga = out_norm[:A_WIDTH].reshape(1, A_WIDTH)
    gb = jnp.pad(out_norm[A_WIDTH:A_WIDTH + B_WIDTH].reshape(FOX_HEADS, FOX_DIM),
                 ((0, 0), (0, 64))).reshape(1, 512)
    gc = out_norm[A_WIDTH + B_WIDTH:].reshape(1, C_WIDTH)
    nblk = FFN_HIDDEN // FFN_BLOCK
    wg = w_gu[:, :FFN_HIDDEN].reshape(D_MODEL, nblk, FFN_BLOCK).transpose(1, 0, 2).astype(BF16)
    wu = w_gu[:, FFN_HIDDEN:].reshape(D_MODEL, nblk, FFN_BLOCK).transpose(1, 0, 2).astype(BF16)
    wd = w_down.reshape(nblk, FFN_BLOCK, D_MODEL).astype(BF16)
    fb = _pad_cols(f_bias.reshape(1, FOX_HEADS), LANES)
    return w_in_r, wuq, wukv_r, woa, wob, woc, ga, gb, gc, wg, wu, wd, fb


def _rope_tables(seq):
    pos = jnp.arange(seq, dtype=F32)
    inv = ROPE_THETA ** (-jnp.arange(0, MLA_ROPE, 2, dtype=F32) / MLA_ROPE)
    ang = pos[:, None] * inv[None, :]
    cos, sin = jnp.cos(ang), jnp.sin(ang)
    z = lambda n: jnp.zeros((seq, n), F32)
    half = MLA_ROPE // 2
    tcos = jnp.concatenate([z(MLA_NOPE), cos, cos, z(32)], axis=1)
    ts1 = jnp.concatenate([z(MLA_NOPE), -sin, z(half), z(32)], axis=1)
    ts2 = jnp.concatenate([z(MLA_NOPE), z(half), sin, z(32)], axis=1)
    return tcos, ts1, ts2


def _tile_bias(bias):
    nq = CHK_TILE // CHUNK
    rows = [jnp.pad(bias, ((0, 0), (0, 0), (c * CHUNK, CHK_WIN - BAND - c * CHUNK)),
                    constant_values=NEG) for c in range(nq)]
    return jnp.concatenate(rows, axis=1)


def kernel(x, mem, norm_mix, w_in, q_norm, w_uq, kv_norm, w_ukv, f_bias, rel_bias, out_norm, w_o,
           norm_cross, norm_mem, w_cq, w_ckv, w_co, norm_ffn, w_gu, w_down, final_norm):
    b, s, _ = x.shape
    depth = w_in.shape[0]
    tcos, ts1, ts2 = _rope_tables(s)
    tri = jnp.tril(jnp.ones((TOK_TILE, TOK_TILE), BF16))
    row = lambda v: v.reshape(1, -1)
    for l in range(depth):
        (w_in_r, wuq, wukv_r, woa, wob, woc, ga, gb, gc, wg, wu, wd, fb) = _layer_weights(
            w_in[l], w_uq[l], w_ukv[l], w_o[l], out_norm[l], w_gu[l], w_down[l], f_bias[l])
        mq, mk, mv, fq, fk, fv, cq, ck, cv, cum = _in_proj(
            x, row(norm_mix[l]), w_in_r, row(q_norm[l]), wuq, row(kv_norm[l]), wukv_r,
            tcos, ts1, ts2, fb, tri)
        ya = _sweep(mq, mk, mv, chunk_mask=True)
        cum_h = jnp.transpose(cum[:, :, :FOX_HEADS], (0, 2, 1))
        yb = _sweep(fq, fk, fv, cum_h.reshape(b, FOX_HEADS, s, 1),
                    cum_h.reshape(b, FOX_HEADS, s // ATT_TILE, ATT_TILE), chunk_mask=False)
        pad = ((0, 0), (CHK_LEFT * CHUNK, 0), (0, 0))
        yc = _chunk_attn(cq, jnp.pad(ck, pad), jnp.pad(cv, pad), _tile_bias(_rel_bias(rel_bias[l])))
        km, vm = _mem_kv(mem, row(norm_mem[l]), w_ckv[l].astype(BF16))
        x = _mix_cross(x, ya, yb, yc, ga, gb, gc, woa, wob, woc, row(norm_cross[l]),
                       w_cq[l].astype(BF16), km, vm, w_co[l].astype(BF16))
        x = _ffn(x, row(norm_ffn[l]), wg, wu, wd, row(final_norm), final_norm=(l == depth - 1))
    return x
```

```python
import functools
import math

import jax
import jax.numpy as jnp
from jax import lax
from jax.experimental import pallas as pl
from jax.experimental.pallas import tpu as pltpu

D_MODEL = 1024
CHUNK = 64
MEM_LEN = 256
EPS = 1e-6
MLA_HEADS = 4
MLA_Q_RANK = 256
MLA_KV_RANK = 128
MLA_NOPE = 64
MLA_ROPE = 32
MLA_V = 128
ROPE_THETA = 10000.0
FOX_HEADS = 4
FOX_DIM = 64
CHK_HEADS = 4
CHK_DIM = 64
CHK_LEFT = 8
BAND = (CHK_LEFT + 1) * CHUNK
REL_MAX = 128
REL_SIZE = (CHUNK - 1) + REL_MAX + 1
A_WIDTH = MLA_HEADS * MLA_V
B_WIDTH = FOX_HEADS * FOX_DIM
C_WIDTH = CHK_HEADS * CHK_DIM
CROSS_HEADS = 4
CROSS_DIM = 128
CROSS_WIDTH = CROSS_HEADS * CROSS_DIM
FFN_HIDDEN = 2816

LANES = 128
LOG2E = 1.4426950408889634
NEG = -1e30
SKIP_LOG2 = 152.0
VMEM_LIMIT = 56 * 1024 * 1024

TOK_TILE = 512
ATT_TILE = 512
SWEEP_UNROLL = 4
CHK_TILE = 256
CHK_WIN = CHK_TILE + CHK_LEFT * CHUNK
FFN_BLOCK = 256
IN_COLS = 2048

BF16 = jnp.bfloat16
F32 = jnp.float32
NT_DIMS = (((1,), (1,)), ((), ()))


def _params(*semantics):
    return pltpu.CompilerParams(dimension_semantics=semantics, vmem_limit_bytes=VMEM_LIMIT)


def _rms(xf, gain, width):
    return xf * lax.rsqrt(jnp.sum(xf * xf, axis=-1, keepdims=True) * (1.0 / width) + EPS) * gain


def _lane_iota(shape):
    return lax.broadcasted_iota(jnp.int32, shape, len(shape) - 1)


def _const_spec(shape):
    zeros = (0,) * len(shape)
    return pl.BlockSpec(shape, lambda *_: zeros)


def _in_proj_kernel(x_ref, g_ref, w_ref, qn_ref, wuq_ref, kvn_ref, wukv_ref, tc_ref, ts1_ref,
                    ts2_ref, fb_ref, tri_ref,
                    mq_ref, mk_ref, mv_ref, fq_ref, fk_ref, fv_ref, cq_ref, ck_ref, cv_ref,
                    cum_ref, carry_ref):
    i = pl.program_id(1)
    tm = x_ref.shape[1]
    h = _rms(x_ref[0], g_ref[...], D_MODEL).astype(BF16)

    def proj(lo, hi):
        return jnp.dot(h, w_ref[:, lo:hi], preferred_element_type=F32)

    lane = _lane_iota((tm, LANES))
    tcos, ts1, ts2 = tc_ref[...], ts1_ref[...], ts2_ref[...]

    def rope(y, keep_low):
        base = tcos + jnp.where(lane < MLA_NOPE, 1.0, 0.0) if keep_low else tcos
        return (y * base + pltpu.roll(y, LANES - MLA_ROPE // 2, 1) * ts1
                + pltpu.roll(y, MLA_ROPE // 2, 1) * ts2)

    c_q = proj(0, 256)
    q_up = jnp.dot(_rms(c_q, qn_ref[...], MLA_Q_RANK).astype(BF16), wuq_ref[...],
                   preferred_element_type=F32)
    misc = proj(384, 512)
    k_pe = rope(misc, False)
    c_kv = proj(256, 384)
    kv_up = jnp.dot(_rms(c_kv, kvn_ref[...], MLA_KV_RANK).astype(BF16), wukv_ref[...],
                    preferred_element_type=F32)
    q_scale = (MLA_NOPE + MLA_ROPE) ** -0.5 * LOG2E
    for hd in range(MLA_HEADS):
        sl = slice(hd * LANES, (hd + 1) * LANES)
        mq_ref[0, hd] = (rope(q_up[:, sl], True) * q_scale).astype(BF16)
        mk_ref[0, hd] = (kv_up[:, sl] + k_pe).astype(BF16)
        mv_ref[0, hd] = kv_up[:, A_WIDTH + hd * LANES:A_WIDTH + (hd + 1) * LANES].astype(BF16)

    def head_block(y, hd):
        blk = y[:, (hd // 2) * LANES:(hd // 2 + 1) * LANES]
        return pltpu.roll(blk, FOX_DIM, 1) if hd % 2 else blk

    fq = proj(512, 768)
    fk = proj(768, 1024)
    fv = proj(1024, 1280)
    f_scale = FOX_DIM ** -0.5 * LOG2E
    ones_col = jnp.where(lane == FOX_DIM, 1.0, 0.0)
    for hd in range(FOX_HEADS):
        fq_ref[0, hd] = jnp.where(lane < FOX_DIM, head_block(fq, hd) * f_scale, 0.0).astype(BF16)
        fk_ref[0, hd] = jnp.where(lane < FOX_DIM, head_block(fk, hd), 0.0).astype(BF16)
        fv_ref[0, hd] = jnp.where(lane < FOX_DIM, head_block(fv, hd), ones_col).astype(BF16)

    z = misc + fb_ref[...]
    log_f = jnp.minimum(z, 0.0) - jnp.log1p(jnp.exp(-jnp.abs(z)))
    log_f = jnp.where(lane < FOX_HEADS, log_f, 0.0)
    p_hi = log_f.astype(BF16)
    rem = log_f - p_hi.astype(F32)
    p_mid = rem.astype(BF16)
    p_lo = (rem - p_mid.astype(F32)).astype(BF16)
    parts = jnp.dot(tri_ref[...], jnp.concatenate([p_hi, p_mid, p_lo], axis=-1),
                    preferred_element_type=F32)
    cum = parts[:, :LANES] + parts[:, LANES:2 * LANES] + parts[:, 2 * LANES:]

    @pl.when(i == 0)
    def _():
        carry_ref[...] = jnp.zeros_like(carry_ref)

    cum = cum + carry_ref[...]
    carry_ref[...] = cum[tm - 1:tm, :]
    cum_ref[0] = cum * LOG2E

    cq_ref[0] = (proj(1280, 1536) * (CHK_DIM ** -0.5 * LOG2E)).astype(BF16)
    ck_ref[0] = proj(1536, 1792).astype(BF16)
    cv_ref[0] = proj(1792, 2048).astype(BF16)


def _in_proj(x, g, w, qn, wuq, kvn, wukv, tcos, ts1, ts2, fb, tri):
    b, s, _ = x.shape
    tm = TOK_TILE
    head_spec = pl.BlockSpec((1, 4, tm, LANES), lambda bi, i: (bi, 0, i, 0))
    dense_spec = pl.BlockSpec((1, tm, 256), lambda bi, i: (bi, i, 0))
    tab_spec = pl.BlockSpec((tm, LANES), lambda bi, i: (i, 0))
    head_shape = jax.ShapeDtypeStruct((b, 4, s, LANES), BF16)
    dense_shape = jax.ShapeDtypeStruct((b, s, 256), BF16)
    return pl.pallas_call(
        _in_proj_kernel,
        grid=(b, s // tm),
        in_specs=[pl.BlockSpec((1, tm, D_MODEL), lambda bi, i: (bi, i, 0)),
                  _const_spec((1, D_MODEL)), _const_spec((D_MODEL, IN_COLS)),
                  _const_spec((1, MLA_Q_RANK)), _const_spec((MLA_Q_RANK, 512)),
                  _const_spec((1, MLA_KV_RANK)), _const_spec((MLA_KV_RANK, 1024)),
                  tab_spec, tab_spec, tab_spec, _const_spec((1, LANES)), _const_spec((tm, tm))],
        out_specs=[head_spec] * 6 + [dense_spec] * 3
                  + [pl.BlockSpec((1, tm, LANES), lambda bi, i: (bi, i, 0))],
        out_shape=[head_shape] * 6 + [dense_shape] * 3
                  + [jax.ShapeDtypeStruct((b, s, LANES), F32)],
        scratch_shapes=[pltpu.VMEM((1, LANES), F32)],
        compiler_params=_params("arbitrary", "arbitrary"),
        name="in_proj",
    )(x, g, w, qn, wuq, kvn, wukv, tcos, ts1, ts2, fb, tri)


def _sweep_kernel(*refs, decay, chunk_mask):
    if decay:
        (q_ref, k_ref, v_ref, cq_ref, ck_ref, o_ref,
         m_sc, l_sc, acc_sc, s_sc, p_sc, alpha_sc, kmax_sc, ckmin_sc) = refs
    else:
        q_ref, k_ref, v_ref, o_ref, m_sc, l_sc, acc_sc, s_sc, p_sc, alpha_sc = refs
    i = pl.program_id(2)
    t = ATT_TILE
    q = q_ref[0, 0]
    m_sc[...] = jnp.full_like(m_sc, NEG)
    l_sc[...] = jnp.zeros_like(l_sc)
    acc_sc[...] = jnp.zeros_like(acc_sc)
    if decay:
        nblk = ck_ref.shape[2]

        @pl.when(i == 0)
        def _():
            def key_norm(j, acc):
                kf = k_ref[0, 0, pl.ds(pl.multiple_of(j * t, t), t), :].astype(F32)
                return jnp.maximum(acc, jnp.max(jnp.sum(kf * kf, axis=-1, keepdims=True),
                                                axis=0, keepdims=True))
            kmax_sc[...] = jnp.sqrt(lax.fori_loop(0, nblk, key_norm, jnp.zeros((1, 1), F32)))
            ckmin_sc[...] = jnp.min(ck_ref[0, 0], axis=-1, keepdims=True)

        cq = cq_ref[0, 0]
        cq_rep = jnp.broadcast_to(cq, (t, LANES))

    def first_needed_block():
        if not decay:
            return 0
        qf = q.astype(F32)
        kd = k_ref[0, 0, pl.ds(pl.multiple_of(i * t, t), t), :].astype(F32)
        gap = (jnp.sqrt(jnp.sum(qf * qf, axis=-1, keepdims=True)) * kmax_sc[...]
               - jnp.sum(qf * kd, axis=-1, keepdims=True) + cq)
        gap_max = jnp.max(gap, axis=0, keepdims=True)
        needed = gap_max - ckmin_sc[...] >= -SKIP_LOG2
        blk = lax.broadcasted_iota(jnp.int32, (nblk, 1), 0)
        return jnp.min(jnp.where(needed, blk, i))

    def scores(j, slot):
        off = pl.multiple_of(j * t, t)
        k = k_ref[0, 0, pl.ds(off, t), :]
        s_sc[slot] = lax.dot_general(q, k, NT_DIMS, preferred_element_type=F32)

    def softmax(j, slot, masked):
        s = s_sc[slot]
        if decay:
            s = s - ck_ref[0, 0, pl.ds(j, 1), :]
        if masked:
            qi = lax.broadcasted_iota(jnp.int32, (t, t), 0)
            ki = lax.broadcasted_iota(jnp.int32, (t, t), 1)
            if chunk_mask:
                qi, ki = qi // CHUNK, ki // CHUNK
            s = jnp.where(ki <= qi, s, NEG)
        m_old = m_sc[...]
        m_cur = jnp.max(s, axis=-1, keepdims=True)
        if decay:
            m_cur = m_cur + cq_rep
        m_new = jnp.maximum(m_old, m_cur)
        alpha = jnp.exp2(m_old - m_new)
        shift = m_new - cq_rep if decay else m_new
        ps = [jnp.exp2(s[:, c * LANES:(c + 1) * LANES] - shift) for c in range(t // LANES)]
        if not decay:
            l_sc[...] = alpha * l_sc[...] + functools.reduce(lambda a, b: a + b, ps)
        p_sc[slot] = jnp.concatenate(ps, axis=-1).astype(BF16)
        alpha_sc[slot] = alpha
        m_sc[...] = m_new

    def weigh(j, slot):
        off = pl.multiple_of(j * t, t)
        v = v_ref[0, 0, pl.ds(off, t), :]
        acc_sc[...] = alpha_sc[slot] * acc_sc[...] + jnp.dot(p_sc[slot], v,
                                                              preferred_element_type=F32)

    scores(i, 0)
    first = first_needed_block()
    n_full = i - first
    scores(first, 1)
    softmax(i, 0, True)
    unroll = SWEEP_UNROLL
    trips = jnp.maximum(n_full - 1, 0) // unroll

    def body(tt, carry):
        j = first + unroll * tt
        weigh(jnp.where(tt == 0, i, j - 1), 0)
        for u in range(unroll):
            scores(j + u + 1, u % 2)
            softmax(j + u, (u + 1) % 2, False)
            if u + 1 < unroll:
                weigh(j + u, (u + 1) % 2)
        return carry

    lax.fori_loop(0, trips, body, 0)
    left = n_full - unroll * trips
    j_left = first + unroll * trips
    j_pending = jnp.where(trips == 0, i, j_left - 1)

    def tail(count):
        weigh(j_pending, 0)
        for u in range(count):
            if u + 1 < count:
                scores(j_left + u + 1, u % 2)
            softmax(j_left + u, (u + 1) % 2, False)
            weigh(j_left + u, (u + 1) % 2)

    for count in range(unroll + 1):
        pl.when(left == count)(functools.partial(tail, count))

    acc = acc_sc[...]
    if decay:
        denom = acc[:, FOX_DIM:FOX_DIM + 1]
    else:
        denom = jnp.sum(l_sc[...], axis=-1, keepdims=True)
    o_ref[0, 0] = acc / denom


def _sweep(q, k, v, cq=None, ck=None, *, chunk_mask):
    b, nh, s, _ = q.shape
    t = ATT_TILE
    decay = cq is not None
    q_spec = pl.BlockSpec((1, 1, t, LANES), lambda bi, hi, i: (bi, hi, i, 0))
    kv_spec = pl.BlockSpec((1, 1, s, LANES), lambda bi, hi, i: (bi, hi, 0, 0))
    in_specs = [q_spec, kv_spec, kv_spec]
    args = [q, k, v]
    scratch = [pltpu.VMEM((t, LANES), F32), pltpu.VMEM((t, LANES), F32),
               pltpu.VMEM((t, LANES), F32), pltpu.VMEM((2, t, t), F32),
               pltpu.VMEM((2, t, t), BF16), pltpu.VMEM((2, t, LANES), F32)]
    if decay:
        in_specs += [pl.BlockSpec((1, 1, t, 1), lambda bi, hi, i: (bi, hi, i, 0)),
                     pl.BlockSpec((1, 1, s // t, t), lambda bi, hi, i: (bi, hi, 0, 0))]
        args += [cq, ck]
        scratch += [pltpu.VMEM((1, 1), F32), pltpu.VMEM((s // t, 1), F32)]
    return pl.pallas_call(
        functools.partial(_sweep_kernel, decay=decay, chunk_mask=chunk_mask),
        grid=(b, nh, s // t),
        in_specs=in_specs,
        out_specs=q_spec,
        out_shape=jax.ShapeDtypeStruct((b, nh, s, LANES), F32),
        scratch_shapes=scratch,
        compiler_params=_params("arbitrary", "arbitrary", "arbitrary"),
        name="fox_sweep" if decay else "mla_sweep",
    )(*args)


def _rel_bias_kernel(tab_ref, o_ref):
    qi = lax.broadcasted_iota(jnp.int32, (CHUNK, BAND), 0)
    ki = lax.broadcasted_iota(jnp.int32, (CHUNK, BAND), 1)
    idx = jnp.clip(qi + CHK_LEFT * CHUNK - ki, -(CHUNK - 1), REL_MAX) + (CHUNK - 1)
    for hd in range(CHK_HEADS):
        def body(r, acc):
            return jnp.where(idx == r, tab_ref[hd, r], acc)
        o_ref[hd] = lax.fori_loop(0, REL_SIZE, body, jnp.zeros((CHUNK, BAND), F32)) * LOG2E


def _rel_bias(table):
    return pl.pallas_call(
        _rel_bias_kernel,
        in_specs=[pl.BlockSpec(memory_space=pltpu.SMEM)],
        out_specs=pl.BlockSpec(memory_space=pltpu.VMEM),
        out_shape=jax.ShapeDtypeStruct((CHK_HEADS, CHUNK, BAND), F32),
        name="rel_bias",
    )(table)


def _chunk_kernel(q_ref, k_ref, v_ref, bias_ref, o_ref):
    i = pl.program_id(1)
    start = pl.multiple_of(jnp.maximum(i * CHK_TILE - CHK_LEFT * CHUNK, 0), CHK_TILE)
    q = q_ref[0]
    kw = k_ref[0, pl.ds(start, CHK_WIN), :]
    vw = v_ref[0, pl.ds(start, CHK_WIN), :]
    heads = [slice(hd * CHK_DIM, (hd + 1) * CHK_DIM) for hd in range(CHK_HEADS)]
    scores = [lax.dot_general(q[:, sl], kw[:, sl], NT_DIMS, preferred_element_type=F32)
              for sl in heads]
    outs = []
    for hd, sl in enumerate(heads):
        s = scores[hd] + bias_ref[0, hd]
        m = jnp.max(s, axis=-1, keepdims=True)
        p = jnp.exp2(s - m)
        denom = jnp.sum(p, axis=-1, keepdims=True)
        outs.append(jnp.dot(p.astype(BF16), vw[:, sl], preferred_element_type=F32) / denom)
    o_ref[0] = jnp.concatenate(outs, axis=-1)


def _chunk_attn(q, k, v, bias):
    b, s, _ = q.shape
    lead = CHK_LEFT * CHUNK // CHK_TILE
    return pl.pallas_call(
        _chunk_kernel,
        grid=(b, s // CHK_TILE),
        in_specs=[pl.BlockSpec((1, CHK_TILE, C_WIDTH), lambda bi, i: (bi, i, 0)),
                  pl.BlockSpec((1, s, C_WIDTH), lambda bi, i: (bi, 0, 0)),
                  pl.BlockSpec((1, s, C_WIDTH), lambda bi, i: (bi, 0, 0)),
                  pl.BlockSpec((1, CHK_HEADS, CHK_TILE, CHK_WIN),
                               lambda bi, i: (jnp.maximum(lead - i, 0), 0, 0, 0))],
        out_specs=pl.BlockSpec((1, CHK_TILE, C_WIDTH), lambda bi, i: (bi, i, 0)),
        out_shape=jax.ShapeDtypeStruct((b, s, C_WIDTH), F32),
        compiler_params=_params("arbitrary", "arbitrary"),
        name="chunk_attn",
    )(q, k, v, bias)


def _mem_kernel(m_ref, g_ref, w_ref, k_ref, v_ref):
    h = _rms(m_ref[0], g_ref[...], D_MODEL).astype(BF16)
    kv = jnp.dot(h, w_ref[...], preferred_element_type=F32)
    k_ref[0] = kv[:, :CROSS_WIDTH].astype(BF16)
    v_ref[0] = kv[:, CROSS_WIDTH:].astype(BF16)


def _mem_kv(mem, g, w):
    b = mem.shape[0]
    spec = pl.BlockSpec((1, MEM_LEN, CROSS_WIDTH), lambda bi: (bi, 0, 0))
    shape = jax.ShapeDtypeStruct((b, MEM_LEN, CROSS_WIDTH), BF16)
    return pl.pallas_call(
        _mem_kernel,
        grid=(b,),
        in_specs=[pl.BlockSpec((1, MEM_LEN, D_MODEL), lambda bi: (bi, 0, 0)),
                  _const_spec((1, D_MODEL)), _const_spec((D_MODEL, 2 * CROSS_WIDTH))],
        out_specs=[spec, spec],
        out_shape=[shape, shape],
        compiler_params=_params("arbitrary"),
        name="mem_kv",
    )(mem, g, w)


def _mix_cross_kernel(x_ref, ya_ref, yb_ref, yc_ref, ga_ref, gb_ref, gc_ref, woa_ref, wob_ref,
                      woc_ref, gx_ref, wcq_ref, km_ref, vm_ref, wco_ref, o_ref):
    tm = x_ref.shape[1]
    lane = _lane_iota((tm, LANES))
    ya = jnp.concatenate([ya_ref[0, hd] for hd in range(MLA_HEADS)], axis=-1)
    yb = jnp.concatenate([jnp.where(lane < FOX_DIM, yb_ref[0, hd], 0.0)
                          for hd in range(FOX_HEADS)], axis=-1)
    x1 = (x_ref[0]
          + jnp.dot(_rms(ya, ga_ref[...], A_WIDTH).astype(BF16), woa_ref[...],
                    preferred_element_type=F32)
          + jnp.dot(_rms(yb, gb_ref[...], B_WIDTH).astype(BF16), wob_ref[...],
                    preferred_element_type=F32)
          + jnp.dot(_rms(yc_ref[0], gc_ref[...], C_WIDTH).astype(BF16), woc_ref[...],
                    preferred_element_type=F32))
    h = _rms(x1, gx_ref[...], D_MODEL).astype(BF16)
    q = (jnp.dot(h, wcq_ref[...], preferred_element_type=F32)
         * (CROSS_DIM ** -0.5 * LOG2E)).astype(BF16)
    outs = []
    for hd in range(CROSS_HEADS):
        sl = slice(hd * CROSS_DIM, (hd + 1) * CROSS_DIM)
        s = lax.dot_general(q[:, sl], km_ref[0, :, sl], NT_DIMS, preferred_element_type=F32)
        m = jnp.max(s, axis=-1, keepdims=True)
        p = jnp.exp2(s - m)
        denom = jnp.sum(p, axis=-1, keepdims=True)
        outs.append(jnp.dot(p.astype(BF16), vm_ref[0, :, sl], preferred_element_type=F32)
                    / denom)
    o = jnp.concatenate(outs, axis=-1).astype(BF16)
    o_ref[0] = x1 + jnp.dot(o, wco_ref[...], preferred_element_type=F32)


def _mix_cross(x, ya, yb, yc, ga, gb, gc, woa, wob, woc, gx, wcq, km, vm, wco):
    b, s, _ = x.shape
    tm = TOK_TILE
    x_spec = pl.BlockSpec((1, tm, D_MODEL), lambda bi, i: (bi, i, 0))
    head_spec = pl.BlockSpec((1, 4, tm, LANES), lambda bi, i: (bi, 0, i, 0))
    mem_spec = pl.BlockSpec((1, MEM_LEN, CROSS_WIDTH), lambda bi, i: (bi, 0, 0))
    return pl.pallas_call(
        _mix_cross_kernel,
        grid=(b, s // tm),
        in_specs=[x_spec, head_spec, head_spec,
                  pl.BlockSpec((1, tm, C_WIDTH), lambda bi, i: (bi, i, 0)),
                  _const_spec((1, A_WIDTH)), _const_spec((1, 512)), _const_spec((1, C_WIDTH)),
                  _const_spec((A_WIDTH, D_MODEL)), _const_spec((512, D_MODEL)),
                  _const_spec((C_WIDTH, D_MODEL)), _const_spec((1, D_MODEL)),
                  _const_spec((D_MODEL, CROSS_WIDTH)), mem_spec, mem_spec,
                  _const_spec((CROSS_WIDTH, D_MODEL))],
        out_specs=x_spec,
        out_shape=jax.ShapeDtypeStruct(x.shape, F32),
        compiler_params=_params("arbitrary", "arbitrary"),
        name="mix_cross",
    )(x, ya, yb, yc, ga, gb, gc, woa, wob, woc, gx, wcq, km, vm, wco)


def _ffn_kernel(x_ref, g_ref, wgu_ref, wd_ref, gf_ref, o_ref, acc_ref, *, final_norm):
    x = x_ref[0]
    h = _rms(x, g_ref[...], D_MODEL).astype(BF16)
    acc_ref[...] = x
    nblk = FFN_HIDDEN // FFN_BLOCK

    def gate_up(c):
        lo = c * FFN_BLOCK
        gate = jnp.dot(h, wgu_ref[:, lo:lo + FFN_BLOCK], preferred_element_type=F32)
        up = jnp.dot(h, wgu_ref[:, FFN_HIDDEN + lo:FFN_HIDDEN + lo + FFN_BLOCK],
                     preferred_element_type=F32)
        return gate, up

    nxt = gate_up(0)
    for c in range(nblk):
        gate, up = nxt
        if c + 1 < nblk:
            nxt = gate_up(c + 1)
        act = (gate * jax.nn.sigmoid(gate) * up).astype(BF16)
        acc_ref[...] += jnp.dot(act, wd_ref[c * FFN_BLOCK:(c + 1) * FFN_BLOCK, :],
                                preferred_element_type=F32)
    y = acc_ref[...]
    o_ref[0] = _rms(y, gf_ref[...], D_MODEL) if final_norm else y


def _ffn(x, g, wgu, wd, gf, *, final_norm):
    b, s, _ = x.shape
    tm = TOK_TILE
    x_spec = pl.BlockSpec((1, tm, D_MODEL), lambda bi, i: (bi, i, 0))

    def resident(shape):
        return pl.BlockSpec(shape, lambda bi, i: (0, 0), pipeline_mode=pl.Buffered(1))

    return pl.pallas_call(
        functools.partial(_ffn_kernel, final_norm=final_norm),
        grid=(b, s // tm),
        in_specs=[x_spec, _const_spec((1, D_MODEL)),
                  resident((D_MODEL, 2 * FFN_HIDDEN)), resident((FFN_HIDDEN, D_MODEL)),
                  _const_spec((1, D_MODEL))],
        out_specs=x_spec,
        out_shape=jax.ShapeDtypeStruct(x.shape, F32),
        scratch_shapes=[pltpu.VMEM((tm, D_MODEL), F32)],
        compiler_params=_params("arbitrary", "arbitrary"),
        name="ffn",
    )(x, g, wgu, wd, gf)


def _pad_cols(a, width):
    return jnp.pad(a, ((0, 0), (0, width - a.shape[1])))


def _layer_weights(w_in, w_uq, w_ukv, w_o, out_norm, w_gu, w_down, f_bias):
    sizes = (256, 128, 32, 256, 256, 256, 4, 256, 256, 256)
    parts, at = [], 0
    for n in sizes:
        parts.append(w_in[:, at:at + n])
        at += n
    c_q, c_kv, k_rope, fq, fk, fv, f_logit, cq, ck, cv = parts
    zeros = lambda n: jnp.zeros((D_MODEL, n), F32)
    misc = jnp.concatenate([f_logit, zeros(MLA_NOPE - FOX_HEADS), k_rope, zeros(32)], axis=1)
    w_in_r = jnp.concatenate([c_q, c_kv, misc, fq, fk, fv, cq, ck, cv], axis=1).astype(BF16)
    wuq = jnp.pad(w_uq.reshape(MLA_Q_RANK, MLA_HEADS, MLA_NOPE + MLA_ROPE),
                  ((0, 0), (0, 0), (0, 32))).reshape(MLA_Q_RANK, 512).astype(BF16)
    wukv = w_ukv.reshape(MLA_KV_RANK, MLA_HEADS, MLA_NOPE + MLA_V)
    wk = jnp.pad(wukv[:, :, :MLA_NOPE], ((0, 0), (0, 0), (0, 64))).reshape(MLA_KV_RANK, 512)
    wv = wukv[:, :, MLA_NOPE:].reshape(MLA_KV_RANK, 512)
    wukv_r = jnp.concatenate([wk, wv], axis=1).astype(BF16)
    woa = w_o[:A_WIDTH].astype(BF16)
    wob = jnp.pad(w_o[A_WIDTH:A_WIDTH + B_WIDTH].reshape(FOX_HEADS, FOX_DIM, D_MODEL),
                  ((0, 0), (0, 64), (0, 0))).reshape(512, D_MODEL).astype(BF16)
    woc = w_o[A_WIDTH + B_WIDTH:].astype(BF16)
    ga = out_norm[:A_WIDTH].reshape(1, A_WIDTH)
    gb = jnp.pad(out_norm[A_WIDTH:A_WIDTH + B_WIDTH].reshape(FOX_HEADS, FOX_DIM),
                 ((0, 0), (0, 64))).reshape(1, 512)
    gc = out_norm[A_WIDTH + B_WIDTH:].reshape(1, C_WIDTH)
    wgu = w_gu.astype(BF16)
    wd = w_down.astype(BF16)
    fb = _pad_cols(f_bias.reshape(1, FOX_HEADS), LANES)
    return w_in_r, wuq, wukv_r, woa, wob, woc, ga, gb, gc, wgu, wd, fb


def _rope_tables(seq):
    pos = jnp.arange(seq, dtype=F32)
    inv = ROPE_THETA ** (-jnp.arange(0, MLA_ROPE, 2, dtype=F32) / MLA_ROPE)
    ang = pos[:, None] * inv[None, :]
    cos, sin = jnp.cos(ang), jnp.sin(ang)
    z = lambda n: jnp.zeros((seq, n), F32)
    half = MLA_ROPE // 2
    tcos = jnp.concatenate([z(MLA_NOPE), cos, cos, z(32)], axis=1)
    ts1 = jnp.concatenate([z(MLA_NOPE), -sin, z(half), z(32)], axis=1)
    ts2 = jnp.concatenate([z(MLA_NOPE), z(half), sin, z(32)], axis=1)
    return tcos, ts1, ts2


def _tile_bias(bias):
    nq = CHK_TILE // CHUNK
    rows = [jnp.pad(bias, ((0, 0), (0, 0), (c * CHUNK, CHK_WIN - BAND - c * CHUNK)),
                    constant_values=NEG) for c in range(nq)]
    tile = jnp.concatenate(rows, axis=1)
    lead = CHK_LEFT * CHUNK // CHK_TILE
    return jnp.stack([jnp.pad(tile[:, :, v * CHK_TILE:], ((0, 0), (0, 0), (0, v * CHK_TILE)),
                              constant_values=NEG) for v in range(lead + 1)])


def kernel(x, mem, norm_mix, w_in, q_norm, w_uq, kv_norm, w_ukv, f_bias, rel_bias, out_norm, w_o,
           norm_cross, norm_mem, w_cq, w_ckv, w_co, norm_ffn, w_gu, w_down, final_norm):
    b, s, _ = x.shape
    depth = w_in.shape[0]
    tcos, ts1, ts2 = _rope_tables(s)
    tri = jnp.tril(jnp.ones((TOK_TILE, TOK_TILE), BF16))
    row = lambda v: v.reshape(1, -1)
    for l in range(depth):
        (w_in_r, wuq, wukv_r, woa, wob, woc, ga, gb, gc, wgu, wd, fb) = _layer_weights(
            w_in[l], w_uq[l], w_ukv[l], w_o[l], out_norm[l], w_gu[l], w_down[l], f_bias[l])
        mq, mk, mv, fq, fk, fv, cq, ck, cv, cum = _in_proj(
            x, row(norm_mix[l]), w_in_r, row(q_norm[l]), wuq, row(kv_norm[l]), wukv_r,
            tcos, ts1, ts2, fb, tri)
        ya = _sweep(mq, mk, mv, chunk_mask=True)
        cum_h = jnp.transpose(cum[:, :, :FOX_HEADS], (0, 2, 1))
        yb = _sweep(fq, fk, fv, cum_h.reshape(b, FOX_HEADS, s, 1),
                    cum_h.reshape(b, FOX_HEADS, s // ATT_TILE, ATT_TILE), chunk_mask=False)
        yc = _chunk_attn(cq, ck, cv, _tile_bias(_rel_bias(rel_bias[l])))
        km, vm = _mem_kv(mem, row(norm_mem[l]), w_ckv[l].astype(BF16))
        x = _mix_cross(x, ya, yb, yc, ga, gb, gc, woa, wob, woc, row(norm_cross[l]),
                       w_cq[l].astype(BF16), km, vm, w_co[l].astype(BF16))
        x = _ffn(x, row(norm_ffn[l]), wgu, wd, row(final_norm), final_norm=(l == depth - 1))
    return x
```

```python
import functools

import jax
import jax.numpy as jnp
from jax import lax
from jax.experimental import pallas as pl
from jax.experimental.pallas import tpu as pltpu

D_MODEL = 1024
CHUNK = 64
MEM_LEN = 256
EPS = 1e-6
MLA_HEADS = 4
MLA_Q_RANK = 256
MLA_KV_RANK = 128
MLA_NOPE = 64
MLA_ROPE = 32
MLA_V = 128
ROPE_THETA = 10000.0
FOX_HEADS = 4
FOX_DIM = 64
CHK_HEADS = 4
CHK_DIM = 64
CHK_LEFT = 8
BAND = (CHK_LEFT + 1) * CHUNK
REL_MAX = 128
REL_SIZE = (CHUNK - 1) + REL_MAX + 1
A_WIDTH = MLA_HEADS * MLA_V
B_WIDTH = FOX_HEADS * FOX_DIM
C_WIDTH = CHK_HEADS * CHK_DIM
CROSS_HEADS = 4
CROSS_DIM = 128
CROSS_WIDTH = CROSS_HEADS * CROSS_DIM
FFN_HIDDEN = 2816

LANES = 128
LOG2E = 1.4426950408889634
NEG = -1e30
SKIP_LOG2 = 152.0
VMEM_LIMIT = 56 * 1024 * 1024

TOK_TILE = 512
ATT_TILE = 512
SWEEP_UNROLL = 4
CHK_TILE = 256
CHK_WIN = CHK_TILE + CHK_LEFT * CHUNK
CHK_STEP = 4 * CHK_TILE
FFN_BLOCK = 256
IN_COLS = 2048

BF16 = jnp.bfloat16
F32 = jnp.float32
NT_DIMS = (((1,), (1,)), ((), ()))


def _params(*semantics):
    return pltpu.CompilerParams(dimension_semantics=semantics, vmem_limit_bytes=VMEM_LIMIT)


def _rms(xf, gain, width):
    return xf * lax.rsqrt(jnp.sum(xf * xf, axis=-1, keepdims=True) * (1.0 / width) + EPS) * gain


def _lane_iota(shape):
    return lax.broadcasted_iota(jnp.int32, shape, len(shape) - 1)


def _const_spec(shape):
    zeros = (0,) * len(shape)
    return pl.BlockSpec(shape, lambda *_: zeros)


def _layer_spec(shape, layer, block=0, **kwargs):
    index = (layer, block, 0)
    return pl.BlockSpec((None,) + tuple(shape), lambda *_: index, **kwargs)


def _in_proj_kernel(x_ref, g_ref, w_ref, qn_ref, wuq_ref, kvn_ref, wukv_ref, tc_ref, ts1_ref,
                    ts2_ref, fb_ref, tri_ref,
                    mq_ref, mk_ref, mv_ref, fq_ref, fk_ref, fv_ref, cq_ref, ck_ref, cv_ref,
                    cum_ref, carry_ref):
    i = pl.program_id(1)
    tm = x_ref.shape[1]
    h = _rms(x_ref[0], g_ref[...], D_MODEL).astype(BF16)

    def proj(lo, hi):
        return jnp.dot(h, w_ref[:, lo:hi], preferred_element_type=F32)

    lane = _lane_iota((tm, LANES))
    tcos, ts1, ts2 = tc_ref[...], ts1_ref[...], ts2_ref[...]

    def rope(y, keep_low):
        base = tcos + jnp.where(lane < MLA_NOPE, 1.0, 0.0) if keep_low else tcos
        return (y * base + pltpu.roll(y, LANES - MLA_ROPE // 2, 1) * ts1
                + pltpu.roll(y, MLA_ROPE // 2, 1) * ts2)

    c_q = proj(0, 256)
    q_up = jnp.dot(_rms(c_q, qn_ref[...], MLA_Q_RANK).astype(BF16), wuq_ref[...],
                   preferred_element_type=F32)
    misc = proj(384, 512)
    k_pe = rope(misc, False)
    c_kv = proj(256, 384)
    kv_up = jnp.dot(_rms(c_kv, kvn_ref[...], MLA_KV_RANK).astype(BF16), wukv_ref[...],
                    preferred_element_type=F32)
    q_scale = (MLA_NOPE + MLA_ROPE) ** -0.5 * LOG2E
    for hd in range(MLA_HEADS):
        sl = slice(hd * LANES, (hd + 1) * LANES)
        mq_ref[0, hd] = (rope(q_up[:, sl], True) * q_scale).astype(BF16)
        mk_ref[0, hd] = (kv_up[:, sl] + k_pe).astype(BF16)
        mv_ref[0, hd] = kv_up[:, A_WIDTH + hd * LANES:A_WIDTH + (hd + 1) * LANES].astype(BF16)

    def head_block(y, hd):
        blk = y[:, (hd // 2) * LANES:(hd // 2 + 1) * LANES]
        return pltpu.roll(blk, FOX_DIM, 1) if hd % 2 else blk

    fq = proj(512, 768)
    fk = proj(768, 1024)
    fv = proj(1024, 1280)
    f_scale = FOX_DIM ** -0.5 * LOG2E
    ones_col = jnp.where(lane == FOX_DIM, 1.0, 0.0)
    for hd in range(FOX_HEADS):
        fq_ref[0, hd] = jnp.where(lane < FOX_DIM, head_block(fq, hd) * f_scale, 0.0).astype(BF16)
        fk_ref[0, hd] = jnp.where(lane < FOX_DIM, head_block(fk, hd), 0.0).astype(BF16)
        fv_ref[0, hd] = jnp.where(lane < FOX_DIM, head_block(fv, hd), ones_col).astype(BF16)

    z = misc + fb_ref[...]
    log_f = jnp.minimum(z, 0.0) - jnp.log1p(jnp.exp(-jnp.abs(z)))
    log_f = jnp.where(lane < FOX_HEADS, log_f, 0.0)
    p_hi = log_f.astype(BF16)
    rem = log_f - p_hi.astype(F32)
    p_mid = rem.astype(BF16)
    p_lo = (rem - p_mid.astype(F32)).astype(BF16)
    parts = jnp.dot(tri_ref[...], jnp.concatenate([p_hi, p_mid, p_lo], axis=-1),
                    preferred_element_type=F32)
    cum = parts[:, :LANES] + parts[:, LANES:2 * LANES] + parts[:, 2 * LANES:]

    @pl.when(i == 0)
    def _():
        carry_ref[...] = jnp.zeros_like(carry_ref)

    cum = cum + carry_ref[...]
    carry_ref[...] = cum[tm - 1:tm, :]
    cum_ref[0] = cum * LOG2E

    cq_ref[0] = (proj(1280, 1536) * (CHK_DIM ** -0.5 * LOG2E)).astype(BF16)
    ck_ref[0] = proj(1536, 1792).astype(BF16)
    cv_ref[0] = proj(1792, 2048).astype(BF16)


def _in_proj(x, g, w, qn, wuq, kvn, wukv, tcos, ts1, ts2, fb, tri):
    b, s, _ = x.shape
    tm = TOK_TILE
    head_spec = pl.BlockSpec((1, 4, tm, LANES), lambda bi, i: (bi, 0, i, 0))
    dense_spec = pl.BlockSpec((1, tm, 256), lambda bi, i: (bi, i, 0))
    tab_spec = pl.BlockSpec((tm, LANES), lambda bi, i: (i, 0))
    head_shape = jax.ShapeDtypeStruct((b, 4, s, LANES), BF16)
    dense_shape = jax.ShapeDtypeStruct((b, s, 256), BF16)
    return pl.pallas_call(
        _in_proj_kernel,
        grid=(b, s // tm),
        in_specs=[pl.BlockSpec((1, tm, D_MODEL), lambda bi, i: (bi, i, 0)),
                  _const_spec((1, D_MODEL)), _const_spec((D_MODEL, IN_COLS)),
                  _const_spec((1, MLA_Q_RANK)), _const_spec((MLA_Q_RANK, 512)),
                  _const_spec((1, MLA_KV_RANK)), _const_spec((MLA_KV_RANK, 1024)),
                  tab_spec, tab_spec, tab_spec, _const_spec((1, LANES)), _const_spec((tm, tm))],
        out_specs=[head_spec] * 6 + [dense_spec] * 3
                  + [pl.BlockSpec((1, tm, LANES), lambda bi, i: (bi, i, 0))],
        out_shape=[head_shape] * 6 + [dense_shape] * 3
                  + [jax.ShapeDtypeStruct((b, s, LANES), F32)],
        scratch_shapes=[pltpu.VMEM((1, LANES), F32)],
        compiler_params=_params("arbitrary", "arbitrary"),
        name="in_proj",
    )(x, g, w, qn, wuq, kvn, wukv, tcos, ts1, ts2, fb, tri)


def _sweep_kernel(*refs, decay, chunk_mask):
    if decay:
        (q_ref, k_ref, v_ref, cq_ref, ck_ref, o_ref,
         m_sc, l_sc, acc_sc, s_sc, p_sc, alpha_sc, kmax_sc, ckmin_sc) = refs
    else:
        q_ref, k_ref, v_ref, o_ref, m_sc, l_sc, acc_sc, s_sc, p_sc, alpha_sc = refs
    i = pl.program_id(2)
    t = ATT_TILE
    q = q_ref[0, 0]
    m_sc[...] = jnp.full_like(m_sc, NEG)
    l_sc[...] = jnp.zeros_like(l_sc)
    acc_sc[...] = jnp.zeros_like(acc_sc)
    if decay:
        nblk = ck_ref.shape[2]

        @pl.when(i == 0)
        def _():
            def key_norm(j, acc):
                kf = k_ref[0, 0, pl.ds(pl.multiple_of(j * t, t), t), :].astype(F32)
                return jnp.maximum(acc, jnp.max(jnp.sum(kf * kf, axis=-1, keepdims=True),
                                                axis=0, keepdims=True))
            kmax_sc[...] = jnp.sqrt(lax.fori_loop(0, nblk, key_norm, jnp.zeros((1, 1), F32)))
            ckmin_sc[...] = jnp.min(ck_ref[0, 0], axis=-1, keepdims=True)

        head_lane = _lane_iota((t, LANES)) == pl.program_id(1)
        cq = jnp.sum(jnp.where(head_lane, cq_ref[0], 0.0), axis=-1, keepdims=True)
        cq_rep = jnp.broadcast_to(cq, (t, LANES))

    def first_needed_block():
        if not decay:
            return 0
        qf = q.astype(F32)
        kd = k_ref[0, 0, pl.ds(pl.multiple_of(i * t, t), t), :].astype(F32)
        gap = (jnp.sqrt(jnp.sum(qf * qf, axis=-1, keepdims=True)) * kmax_sc[...]
               - jnp.sum(qf * kd, axis=-1, keepdims=True) + cq)
        gap_max = jnp.max(gap, axis=0, keepdims=True)
        needed = gap_max - ckmin_sc[...] >= -SKIP_LOG2
        blk = lax.broadcasted_iota(jnp.int32, (nblk, 1), 0)
        return jnp.min(jnp.where(needed, blk, i))

    def scores(j, slot):
        off = pl.multiple_of(j * t, t)
        k = k_ref[0, 0, pl.ds(off, t), :]
        s_sc[slot] = lax.dot_general(q, k, NT_DIMS, preferred_element_type=F32)

    def softmax(j, slot, masked):
        s = s_sc[slot]
        if decay:
            s = s - ck_ref[0, 0, pl.ds(j, 1), :]
        if masked:
            qi = lax.broadcasted_iota(jnp.int32, (t, t), 0)
            ki = lax.broadcasted_iota(jnp.int32, (t, t), 1)
            if chunk_mask:
                qi, ki = qi // CHUNK, ki // CHUNK
            s = jnp.where(ki <= qi, s, NEG)
        m_old = m_sc[...]
        m_cur = jnp.max(s, axis=-1, keepdims=True)
        if decay:
            m_cur = m_cur + cq_rep
        m_new = jnp.maximum(m_old, m_cur)
        alpha = jnp.exp2(m_old - m_new)
        shift = m_new - cq_rep if decay else m_new
        ps = [jnp.exp2(s[:, c * LANES:(c + 1) * LANES] - shift) for c in range(t // LANES)]
        if not decay:
            l_sc[...] = alpha * l_sc[...] + functools.reduce(lambda a, b: a + b, ps)
        p_sc[slot] = jnp.concatenate(ps, axis=-1).astype(BF16)
        alpha_sc[slot] = alpha
        m_sc[...] = m_new

    def weigh(j, slot):
        off = pl.multiple_of(j * t, t)
        v = v_ref[0, 0, pl.ds(off, t), :]
        acc_sc[...] = alpha_sc[slot] * acc_sc[...] + jnp.dot(p_sc[slot], v,
                                                              preferred_element_type=F32)

    scores(i, 0)
    first = first_needed_block()
    n_full = i - first
    scores(first, 1)
    softmax(i, 0, True)
    unroll = SWEEP_UNROLL
    trips = jnp.maximum(n_full - 1, 0) // unroll

    def body(tt, carry):
        j = first + unroll * tt
        weigh(jnp.where(tt == 0, i, j - 1), 0)
        for u in range(unroll):
            scores(j + u + 1, u % 2)
            softmax(j + u, (u + 1) % 2, False)
            if u + 1 < unroll:
                weigh(j + u, (u + 1) % 2)
        return carry

    lax.fori_loop(0, trips, body, 0)
    left = n_full - unroll * trips
    j_left = first + unroll * trips
    j_pending = jnp.where(trips == 0, i, j_left - 1)

    def tail(count):
        weigh(j_pending, 0)
        for u in range(count):
            if u + 1 < count:
                scores(j_left + u + 1, u % 2)
            softmax(j_left + u, (u + 1) % 2, False)
            weigh(j_left + u, (u + 1) % 2)

    for count in range(unroll + 1):
        pl.when(left == count)(functools.partial(tail, count))

    acc = acc_sc[...]
    if decay:
        denom = acc[:, FOX_DIM:FOX_DIM + 1]
    else:
        denom = jnp.sum(l_sc[...], axis=-1, keepdims=True)
    o_ref[0, 0] = acc / denom


def _sweep(q, k, v, cum=None, ck=None, *, chunk_mask):
    b, nh, s, _ = q.shape
    t = ATT_TILE
    decay = cum is not None
    q_spec = pl.BlockSpec((1, 1, t, LANES), lambda bi, hi, i: (bi, hi, i, 0))
    kv_spec = pl.BlockSpec((1, 1, s, LANES), lambda bi, hi, i: (bi, hi, 0, 0))
    in_specs = [q_spec, kv_spec, kv_spec]
    args = [q, k, v]
    scratch = [pltpu.VMEM((t, LANES), F32), pltpu.VMEM((t, LANES), F32),
               pltpu.VMEM((t, LANES), F32), pltpu.VMEM((2, t, t), F32),
               pltpu.VMEM((2, t, t), BF16), pltpu.VMEM((2, t, LANES), F32)]
    if decay:
        in_specs += [pl.BlockSpec((1, t, LANES), lambda bi, hi, i: (bi, i, 0)),
                     pl.BlockSpec((1, 1, s // t, t), lambda bi, hi, i: (bi, hi, 0, 0))]
        args += [cum, ck]
        scratch += [pltpu.VMEM((1, 1), F32), pltpu.VMEM((s // t, 1), F32)]
    return pl.pallas_call(
        functools.partial(_sweep_kernel, decay=decay, chunk_mask=chunk_mask),
        grid=(b, nh, s // t),
        in_specs=in_specs,
        out_specs=q_spec,
        out_shape=jax.ShapeDtypeStruct((b, nh, s, LANES), F32),
        scratch_shapes=scratch,
        compiler_params=_params("arbitrary", "arbitrary", "arbitrary"),
        name="fox_sweep" if decay else "mla_sweep",
    )(*args)


def _rel_bias_kernel(tab_ref, o_ref):
    qi = lax.broadcasted_iota(jnp.int32, (CHUNK, BAND), 0)
    ki = lax.broadcasted_iota(jnp.int32, (CHUNK, BAND), 1)
    idx = jnp.clip(qi + CHK_LEFT * CHUNK - ki, -(CHUNK - 1), REL_MAX) + (CHUNK - 1)
    for hd in range(CHK_HEADS):
        def body(r, acc):
            return jnp.where(idx == r, tab_ref[hd, r], acc)
        o_ref[hd] = lax.fori_loop(0, REL_SIZE, body, jnp.zeros((CHUNK, BAND), F32)) * LOG2E


def _rel_bias(table):
    return pl.pallas_call(
        _rel_bias_kernel,
        in_specs=[pl.BlockSpec(memory_space=pltpu.SMEM)],
        out_specs=pl.BlockSpec(memory_space=pltpu.VMEM),
        out_shape=jax.ShapeDtypeStruct((CHK_HEADS, CHUNK, BAND), F32),
        name="rel_bias",
    )(table)


def _chunk_kernel(q_ref, k_ref, v_ref, bias_ref, o_ref):
    lane = _lane_iota((CHK_TILE, C_WIDTH)) // CHK_DIM
    lead = CHK_LEFT * CHUNK // CHK_TILE
    tiles = CHK_STEP // CHK_TILE
    units = []
    for u in range(tiles):
        tile = pl.program_id(1) * tiles + u
        start = pl.multiple_of(jnp.maximum(tile - lead, 0) * CHK_TILE, CHK_TILE)
        shift = jnp.maximum(lead - tile, 0)
        units += [(u, hd, start, shift) for hd in range(CHK_HEADS)]

    def scores(unit):
        u, hd, start, _ = unit
        q = q_ref[0, u * CHK_TILE:(u + 1) * CHK_TILE, :]
        kw = k_ref[0, pl.ds(start, CHK_WIN), :]
        return lax.dot_general(jnp.where(lane == hd, q, jnp.zeros((), q.dtype)), kw, NT_DIMS,
                               preferred_element_type=F32)

    def attend(unit, s):
        _, hd, start, shift = unit
        s = s + bias_ref[shift, hd]
        m = jnp.max(s, axis=-1, keepdims=True)
        p = jnp.exp2(s - m)
        denom = jnp.sum(p, axis=-1, keepdims=True)
        vw = v_ref[0, pl.ds(start, CHK_WIN), :]
        full = jnp.dot(p.astype(BF16), vw, preferred_element_type=F32)
        return jnp.where(lane == hd, full / denom, 0.0)

    outs = [None] * tiles
    s_next = scores(units[0])
    for n, unit in enumerate(units):
        u, hd = unit[:2]
        s_cur = s_next
        if n + 1 < len(units):
            s_next = scores(units[n + 1])
        o = attend(unit, s_cur)
        outs[u] = o if outs[u] is None else outs[u] + o
        if hd == CHK_HEADS - 1:
            o_ref[0, u * CHK_TILE:(u + 1) * CHK_TILE, :] = outs[u]


def _chunk_attn(q, k, v, bias):
    b, s, _ = q.shape
    return pl.pallas_call(
        _chunk_kernel,
        grid=(b, s // CHK_STEP),
        in_specs=[pl.BlockSpec((1, CHK_STEP, C_WIDTH), lambda bi, i: (bi, i, 0)),
                  pl.BlockSpec((1, s, C_WIDTH), lambda bi, i: (bi, 0, 0)),
                  pl.BlockSpec((1, s, C_WIDTH), lambda bi, i: (bi, 0, 0)),
                  pl.BlockSpec(bias.shape, lambda bi, i: (0, 0, 0, 0),
                               pipeline_mode=pl.Buffered(1))],
        out_specs=pl.BlockSpec((1, CHK_STEP, C_WIDTH), lambda bi, i: (bi, i, 0)),
        out_shape=jax.ShapeDtypeStruct((b, s, C_WIDTH), F32),
        compiler_params=_params("arbitrary", "arbitrary"),
        name="chunk_attn",
    )(q, k, v, bias)


def _mem_kernel(m_ref, g_ref, w_ref, k_ref, v_ref):
    h = _rms(m_ref[0], g_ref[...], D_MODEL).astype(BF16)
    kv = jnp.dot(h, w_ref[...], preferred_element_type=F32)
    k_ref[0] = kv[:, :CROSS_WIDTH].astype(BF16)
    v_ref[0] = kv[:, CROSS_WIDTH:].astype(BF16)


def _mem_kv(mem, g, w, layer):
    b = mem.shape[0]
    spec = pl.BlockSpec((1, MEM_LEN, CROSS_WIDTH), lambda bi: (bi, 0, 0))
    shape = jax.ShapeDtypeStruct((b, MEM_LEN, CROSS_WIDTH), BF16)
    return pl.pallas_call(
        _mem_kernel,
        grid=(b,),
        in_specs=[pl.BlockSpec((1, MEM_LEN, D_MODEL), lambda bi: (bi, 0, 0)),
                  _const_spec((1, D_MODEL)), _layer_spec((D_MODEL, 2 * CROSS_WIDTH), layer)],
        out_specs=[spec, spec],
        out_shape=[shape, shape],
        compiler_params=_params("arbitrary"),
        name="mem_kv",
    )(mem, g, w)


def _mix_cross_kernel(x_ref, ya_ref, yb_ref, yc_ref, ga_ref, gb_ref, gc_ref, woa_ref, wob_ref,
                      woc_ref, gx_ref, wcq_ref, km_ref, vm_ref, wco_ref, o_ref):
    tm = x_ref.shape[1]
    lane = _lane_iota((tm, LANES))
    ya = jnp.concatenate([ya_ref[0, hd] for hd in range(MLA_HEADS)], axis=-1)
    yb = jnp.concatenate([jnp.where(lane < FOX_DIM, yb_ref[0, hd], 0.0)
                          for hd in range(FOX_HEADS)], axis=-1)
    x1 = (x_ref[0]
          + jnp.dot(_rms(ya, ga_ref[...], A_WIDTH).astype(BF16), woa_ref[...],
                    preferred_element_type=F32)
          + jnp.dot(_rms(yb, gb_ref[...], B_WIDTH).astype(BF16), wob_ref[...],
                    preferred_element_type=F32)
          + jnp.dot(_rms(yc_ref[0], gc_ref[...], C_WIDTH).astype(BF16), woc_ref[...],
                    preferred_element_type=F32))
    h = _rms(x1, gx_ref[...], D_MODEL).astype(BF16)
    q = (jnp.dot(h, wcq_ref[...], preferred_element_type=F32)
         * (CROSS_DIM ** -0.5 * LOG2E)).astype(BF16)
    outs = []
    for hd in range(CROSS_HEADS):
        sl = slice(hd * CROSS_DIM, (hd + 1) * CROSS_DIM)
        s = lax.dot_general(q[:, sl], km_ref[0, :, sl], NT_DIMS, preferred_element_type=F32)
        m = jnp.max(s, axis=-1, keepdims=True)
        p = jnp.exp2(s - m)
        denom = jnp.sum(p, axis=-1, keepdims=True)
        outs.append(jnp.dot(p.astype(BF16), vm_ref[0, :, sl], preferred_element_type=F32)
                    / denom)
    o = jnp.concatenate(outs, axis=-1).astype(BF16)
    o_ref[0] = x1 + jnp.dot(o, wco_ref[...], preferred_element_type=F32)


def _mix_cross(x, ya, yb, yc, ga, gb, gc, wo, wob, gx, wcq, km, vm, wco, layer):
    b, s, _ = x.shape
    tm = TOK_TILE
    x_spec = pl.BlockSpec((1, tm, D_MODEL), lambda bi, i: (bi, i, 0))
    head_spec = pl.BlockSpec((1, 4, tm, LANES), lambda bi, i: (bi, 0, i, 0))
    mem_spec = pl.BlockSpec((1, MEM_LEN, CROSS_WIDTH), lambda bi, i: (bi, 0, 0))
    return pl.pallas_call(
        _mix_cross_kernel,
        grid=(b, s // tm),
        in_specs=[x_spec, head_spec, head_spec,
                  pl.BlockSpec((1, tm, C_WIDTH), lambda bi, i: (bi, i, 0)),
                  _const_spec((1, A_WIDTH)), _const_spec((1, 512)), _const_spec((1, C_WIDTH)),
                  _layer_spec((A_WIDTH, D_MODEL), layer), _const_spec((512, D_MODEL)),
                  _layer_spec((C_WIDTH, D_MODEL), layer, (A_WIDTH + B_WIDTH) // C_WIDTH),
                  _const_spec((1, D_MODEL)), _layer_spec((D_MODEL, CROSS_WIDTH), layer),
                  mem_spec, mem_spec, _layer_spec((CROSS_WIDTH, D_MODEL), layer)],
        out_specs=x_spec,
        out_shape=jax.ShapeDtypeStruct(x.shape, F32),
        compiler_params=_params("arbitrary", "arbitrary"),
        name="mix_cross",
    )(x, ya, yb, yc, ga, gb, gc, wo, wob, wo, gx, wcq, km, vm, wco)


def _ffn_kernel(x_ref, g_ref, wgu_ref, wd_ref, gf_ref, o_ref, acc_ref, *, final_norm):
    x = x_ref[0]
    h = _rms(x, g_ref[...], D_MODEL).astype(BF16)
    acc_ref[...] = x
    nblk = FFN_HIDDEN // FFN_BLOCK

    def gate_up(c):
        lo = c * FFN_BLOCK
        gate = jnp.dot(h, wgu_ref[:, lo:lo + FFN_BLOCK], preferred_element_type=F32)
        up = jnp.dot(h, wgu_ref[:, FFN_HIDDEN + lo:FFN_HIDDEN + lo + FFN_BLOCK],
                     preferred_element_type=F32)
        return gate, up

    nxt = gate_up(0)
    for c in range(nblk):
        gate, up = nxt
        if c + 1 < nblk:
            nxt = gate_up(c + 1)
        act = (gate * jax.nn.sigmoid(gate) * up).astype(BF16)
        acc_ref[...] += jnp.dot(act, wd_ref[c * FFN_BLOCK:(c + 1) * FFN_BLOCK, :],
                                preferred_element_type=F32)
    y = acc_ref[...]
    o_ref[0] = _rms(y, gf_ref[...], D_MODEL) if final_norm else y


def _ffn(x, g, wgu, wd, gf, layer, *, final_norm):
    b, s, _ = x.shape
    tm = TOK_TILE
    x_spec = pl.BlockSpec((1, tm, D_MODEL), lambda bi, i: (bi, i, 0))

    def resident(shape):
        return _layer_spec(shape, layer, pipeline_mode=pl.Buffered(1))

    return pl.pallas_call(
        functools.partial(_ffn_kernel, final_norm=final_norm),
        grid=(b, s // tm),
        in_specs=[x_spec, _const_spec((1, D_MODEL)),
                  resident((D_MODEL, 2 * FFN_HIDDEN)), resident((FFN_HIDDEN, D_MODEL)),
                  _const_spec((1, D_MODEL))],
        out_specs=x_spec,
        out_shape=jax.ShapeDtypeStruct(x.shape, F32),
        scratch_shapes=[pltpu.VMEM((tm, D_MODEL), F32)],
        compiler_params=_params("arbitrary", "arbitrary"),
        name="ffn",
    )(x, g, wgu, wd, gf)


def _pad_cols(a, width):
    return jnp.pad(a, ((0, 0), (0, width - a.shape[1])))


def _layer_weights(w_in, w_uq, w_ukv, w_o, out_norm, f_bias):
    sizes = (256, 128, 32, 256, 256, 256, 4, 256, 256, 256)
    parts, at = [], 0
    for n in sizes:
        parts.append(w_in[:, at:at + n])
        at += n
    c_q, c_kv, k_rope, fq, fk, fv, f_logit, cq, ck, cv = parts
    zeros = lambda n: jnp.zeros((D_MODEL, n), F32)
    misc = jnp.concatenate([f_logit, zeros(MLA_NOPE - FOX_HEADS), k_rope, zeros(32)], axis=1)
    w_in_r = jnp.concatenate([c_q, c_kv, misc, fq, fk, fv, cq, ck, cv], axis=1).astype(BF16)
    wuq = jnp.pad(w_uq.reshape(MLA_Q_RANK, MLA_HEADS, MLA_NOPE + MLA_ROPE),
                  ((0, 0), (0, 0), (0, 32))).reshape(MLA_Q_RANK, 512).astype(BF16)
    wukv = w_ukv.reshape(MLA_KV_RANK, MLA_HEADS, MLA_NOPE + MLA_V)
    wk = jnp.pad(wukv[:, :, :MLA_NOPE], ((0, 0), (0, 0), (0, 64))).reshape(MLA_KV_RANK, 512)
    wv = wukv[:, :, MLA_NOPE:].reshape(MLA_KV_RANK, 512)
    wukv_r = jnp.concatenate([wk, wv], axis=1).astype(BF16)
    wob = jnp.pad(w_o[A_WIDTH:A_WIDTH + B_WIDTH].reshape(FOX_HEADS, FOX_DIM, D_MODEL),
                  ((0, 0), (0, 64), (0, 0))).reshape(512, D_MODEL).astype(BF16)
    ga = out_norm[:A_WIDTH].reshape(1, A_WIDTH)
    gb = jnp.pad(out_norm[A_WIDTH:A_WIDTH + B_WIDTH].reshape(FOX_HEADS, FOX_DIM),
                 ((0, 0), (0, 64))).reshape(1, 512)
    gc = out_norm[A_WIDTH + B_WIDTH:].reshape(1, C_WIDTH)
    fb = _pad_cols(f_bias.reshape(1, FOX_HEADS), LANES)
    return w_in_r, wuq, wukv_r, wob, ga, gb, gc, fb


def _rope_tables(seq):
    pos = jnp.arange(seq, dtype=F32)
    inv = ROPE_THETA ** (-jnp.arange(0, MLA_ROPE, 2, dtype=F32) / MLA_ROPE)
    ang = pos[:, None] * inv[None, :]
    cos, sin = jnp.cos(ang), jnp.sin(ang)
    z = lambda n: jnp.zeros((seq, n), F32)
    half = MLA_ROPE // 2
    tcos = jnp.concatenate([z(MLA_NOPE), cos, cos, z(32)], axis=1)
    ts1 = jnp.concatenate([z(MLA_NOPE), -sin, z(half), z(32)], axis=1)
    ts2 = jnp.concatenate([z(MLA_NOPE), z(half), sin, z(32)], axis=1)
    return tcos, ts1, ts2


def _tile_bias(bias):
    nq = CHK_TILE // CHUNK
    rows = [jnp.pad(bias, ((0, 0), (0, 0), (c * CHUNK, CHK_WIN - BAND - c * CHUNK)),
                    constant_values=NEG) for c in range(nq)]
    tile = jnp.concatenate(rows, axis=1)
    lead = CHK_LEFT * CHUNK // CHK_TILE
    return jnp.stack([jnp.pad(tile[:, :, v * CHK_TILE:], ((0, 0), (0, 0), (0, v * CHK_TILE)),
                              constant_values=NEG) for v in range(lead + 1)])


def kernel(x, mem, norm_mix, w_in, q_norm, w_uq, kv_norm, w_ukv, f_bias, rel_bias, out_norm, w_o,
           norm_cross, norm_mem, w_cq, w_ckv, w_co, norm_ffn, w_gu, w_down, final_norm):
    b, s, _ = x.shape
    depth = w_in.shape[0]
    tcos, ts1, ts2 = _rope_tables(s)
    tri = jnp.tril(jnp.ones((TOK_TILE, TOK_TILE), BF16))
    row = lambda v: v.reshape(1, -1)
    wo_b, wcq_b, wckv_b, wco_b, wgu_b, wd_b = (w.astype(BF16)
                                                for w in (w_o, w_cq, w_ckv, w_co, w_gu, w_down))
    for l in range(depth):
        (w_in_r, wuq, wukv_r, wob, ga, gb, gc, fb) = _layer_weights(
            w_in[l], w_uq[l], w_ukv[l], w_o[l], out_norm[l], f_bias[l])
        mq, mk, mv, fq, fk, fv, cq, ck, cv, cum = _in_proj(
            x, row(norm_mix[l]), w_in_r, row(q_norm[l]), wuq, row(kv_norm[l]), wukv_r,
            tcos, ts1, ts2, fb, tri)
        ya = _sweep(mq, mk, mv, chunk_mask=True)
        cum_h = jnp.transpose(cum[:, :, :FOX_HEADS], (0, 2, 1))
        yb = _sweep(fq, fk, fv, cum, cum_h.reshape(b, FOX_HEADS, s // ATT_TILE, ATT_TILE),
                    chunk_mask=False)
        yc = _chunk_attn(cq, ck, cv, _tile_bias(_rel_bias(rel_bias[l])))
        km, vm = _mem_kv(mem, row(norm_mem[l]), wckv_b, l)
        x = _mix_cross(x, ya, yb, yc, ga, gb, gc, wo_b, wob, row(norm_cross[l]), wcq_b, km, vm,
                       wco_b, l)
        x = _ffn(x, row(norm_ffn[l]), wgu_b, wd_b, row(final_norm), l,
                 final_norm=(l == depth - 1))
    return x
```

```python
import functools

import jax
import jax.numpy as jnp
from jax import lax
from jax.experimental import pallas as pl
from jax.experimental.pallas import tpu as pltpu

D_MODEL = 1024
CHUNK = 64
MEM_LEN = 256
EPS = 1e-6
MLA_HEADS = 4
MLA_Q_RANK = 256
MLA_KV_RANK = 128
MLA_NOPE = 64
MLA_ROPE = 32
MLA_V = 128
ROPE_THETA = 10000.0
FOX_HEADS = 4
FOX_DIM = 64
CHK_HEADS = 4
CHK_DIM = 64
CHK_LEFT = 8
BAND = (CHK_LEFT + 1) * CHUNK
REL_MAX = 128
REL_SIZE = (CHUNK - 1) + REL_MAX + 1
A_WIDTH = MLA_HEADS * MLA_V
B_WIDTH = FOX_HEADS * FOX_DIM
C_WIDTH = CHK_HEADS * CHK_DIM
CROSS_HEADS = 4
CROSS_DIM = 128
CROSS_WIDTH = CROSS_HEADS * CROSS_DIM
FFN_HIDDEN = 2816

LANES = 128
LOG2E = 1.4426950408889634
NEG = -1e30
SKIP_LOG2 = 152.0
VMEM_LIMIT = 56 * 1024 * 1024

TOK_TILE = 512
ATT_TILE = 512
SWEEP_UNROLL = 4
CHK_TILE = 256
CHK_WIN = CHK_TILE + CHK_LEFT * CHUNK
CHK_STEP = 4 * CHK_TILE
FFN_BLOCK = 256
IN_COLS = 2048
IN_SPLIT = 2
MIX_SPLIT = 2

BF16 = jnp.bfloat16
F32 = jnp.float32
NT_DIMS = (((1,), (1,)), ((), ()))


def _params(*semantics):
    return pltpu.CompilerParams(dimension_semantics=semantics, vmem_limit_bytes=VMEM_LIMIT)


def _rms(xf, gain, width):
    return xf * lax.rsqrt(jnp.sum(xf * xf, axis=-1, keepdims=True) * (1.0 / width) + EPS) * gain


def _lane_iota(shape):
    return lax.broadcasted_iota(jnp.int32, shape, len(shape) - 1)


def _const_spec(shape):
    zeros = (0,) * len(shape)
    return pl.BlockSpec(shape, lambda *_: zeros)


def _layer_spec(shape, layer, block=0, **kwargs):
    index = (layer, block, 0)
    return pl.BlockSpec((None,) + tuple(shape), lambda *_: index, **kwargs)


def _in_proj_kernel(x_ref, g_ref, w_ref, qn_ref, wuq_ref, kvn_ref, wukv_ref, tc_ref, ts1_ref,
                    ts2_ref, fb_ref, tri_ref,
                    mq_ref, mk_ref, mv_ref, fq_ref, fk_ref, fv_ref, cq_ref, ck_ref, cv_ref,
                    cum_ref, carry_ref):
    i = pl.program_id(1)
    tm = x_ref.shape[1]
    th = tm // IN_SPLIT
    lane = _lane_iota((th, LANES))
    q_scale = (MLA_NOPE + MLA_ROPE) ** -0.5 * LOG2E
    f_scale = FOX_DIM ** -0.5 * LOG2E
    ones_col = jnp.where(lane == FOX_DIM, 1.0, 0.0)

    @pl.when(i == 0)
    def _():
        carry_ref[...] = jnp.zeros_like(carry_ref)

    subs = [dict(rows=slice(r * th, (r + 1) * th)) for r in range(IN_SPLIT)]

    def proj(st, lo, hi):
        return jnp.dot(st["h"], w_ref[:, lo:hi], preferred_element_type=F32)

    def rope(st, y, keep_low):
        rows = st["rows"]
        tcos = tc_ref[rows, :]
        base = tcos + jnp.where(lane < MLA_NOPE, 1.0, 0.0) if keep_low else tcos
        return (y * base + pltpu.roll(y, LANES - MLA_ROPE // 2, 1) * ts1_ref[rows, :]
                + pltpu.roll(y, MLA_ROPE // 2, 1) * ts2_ref[rows, :])

    def head_block(y, hd):
        blk = y[:, (hd // 2) * LANES:(hd // 2 + 1) * LANES]
        return pltpu.roll(blk, FOX_DIM, 1) if hd % 2 else blk

    def stage_norm(st):
        st["h"] = _rms(x_ref[0, st["rows"], :], g_ref[...], D_MODEL).astype(BF16)

    def stage_latent(st):
        st["c_q"] = proj(st, 0, 256)
        st["misc"] = proj(st, 384, 512)
        st["c_kv"] = proj(st, 256, 384)

    def stage_up(st):
        st["q_up"] = jnp.dot(_rms(st.pop("c_q"), qn_ref[...], MLA_Q_RANK).astype(BF16),
                             wuq_ref[...], preferred_element_type=F32)
        st["kv_up"] = jnp.dot(_rms(st.pop("c_kv"), kvn_ref[...], MLA_KV_RANK).astype(BF16),
                              wukv_ref[...], preferred_element_type=F32)

    def stage_fox_proj(st):
        st["fq"] = proj(st, 512, 768)
        st["fk"] = proj(st, 768, 1024)
        st["fv"] = proj(st, 1024, 1280)

    def stage_mla_store(st):
        rows = st["rows"]
        q_up, kv_up = st.pop("q_up"), st.pop("kv_up")
        k_pe = rope(st, st["misc"], False)
        for hd in range(MLA_HEADS):
            sl = slice(hd * LANES, (hd + 1) * LANES)
            mq_ref[0, hd, rows, :] = (rope(st, q_up[:, sl], True) * q_scale).astype(BF16)
            mk_ref[0, hd, rows, :] = (kv_up[:, sl] + k_pe).astype(BF16)
            mv_ref[0, hd, rows, :] = kv_up[:, A_WIDTH + hd * LANES:
                                           A_WIDTH + (hd + 1) * LANES].astype(BF16)

    def stage_chunk_proj(st):
        rows = st["rows"]
        cq_ref[0, rows, :] = (proj(st, 1280, 1536) * (CHK_DIM ** -0.5 * LOG2E)).astype(BF16)
        ck_ref[0, rows, :] = proj(st, 1536, 1792).astype(BF16)
        cv_ref[0, rows, :] = proj(st, 1792, 2048).astype(BF16)

    def stage_fox_store(st):
        rows = st["rows"]
        fq, fk, fv = st.pop("fq"), st.pop("fk"), st.pop("fv")
        for hd in range(FOX_HEADS):
            fq_ref[0, hd, rows, :] = jnp.where(lane < FOX_DIM, head_block(fq, hd) * f_scale,
                                               0.0).astype(BF16)
            fk_ref[0, hd, rows, :] = jnp.where(lane < FOX_DIM, head_block(fk, hd),
                                               0.0).astype(BF16)
            fv_ref[0, hd, rows, :] = jnp.where(lane < FOX_DIM, head_block(fv, hd),
                                               ones_col).astype(BF16)

    def stage_gate(st):
        z = st.pop("misc") + fb_ref[...]
        log_f = jnp.minimum(z, 0.0) - jnp.log1p(jnp.exp(-jnp.abs(z)))
        log_f = jnp.where(lane < FOX_HEADS, log_f, 0.0)
        p_hi = log_f.astype(BF16)
        rem = log_f - p_hi.astype(F32)
        p_mid = rem.astype(BF16)
        p_lo = (rem - p_mid.astype(F32)).astype(BF16)
        parts = jnp.dot(tri_ref[:th, :th], jnp.concatenate([p_hi, p_mid, p_lo], axis=-1),
                        preferred_element_type=F32)
        st["cum"] = parts[:, :LANES] + parts[:, LANES:2 * LANES] + parts[:, 2 * LANES:]

    for stage in (stage_norm, stage_latent, stage_up, stage_fox_proj, stage_mla_store,
                  stage_chunk_proj, stage_fox_store, stage_gate):
        for st in subs:
            stage(st)

    offset = carry_ref[...]
    for st in subs:
        cum = st["cum"] + offset
        offset = cum[th - 1:th, :]
        cum_ref[0, st["rows"], :] = cum * LOG2E
    carry_ref[...] = offset


def _in_proj(x, g, w, qn, wuq, kvn, wukv, tcos, ts1, ts2, fb, tri):
    b, s, _ = x.shape
    tm = TOK_TILE
    head_spec = pl.BlockSpec((1, 4, tm, LANES), lambda bi, i: (bi, 0, i, 0))
    dense_spec = pl.BlockSpec((1, tm, 256), lambda bi, i: (bi, i, 0))
    tab_spec = pl.BlockSpec((tm, LANES), lambda bi, i: (i, 0))
    head_shape = jax.ShapeDtypeStruct((b, 4, s, LANES), BF16)
    dense_shape = jax.ShapeDtypeStruct((b, s, 256), BF16)
    return pl.pallas_call(
        _in_proj_kernel,
        grid=(b, s // tm),
        in_specs=[pl.BlockSpec((1, tm, D_MODEL), lambda bi, i: (bi, i, 0)),
                  _const_spec((1, D_MODEL)), _const_spec((D_MODEL, IN_COLS)),
                  _const_spec((1, MLA_Q_RANK)), _const_spec((MLA_Q_RANK, 512)),
                  _const_spec((1, MLA_KV_RANK)), _const_spec((MLA_KV_RANK, 1024)),
                  tab_spec, tab_spec, tab_spec, _const_spec((1, LANES)), _const_spec((tm, tm))],
        out_specs=[head_spec] * 6 + [dense_spec] * 3
                  + [pl.BlockSpec((1, tm, LANES), lambda bi, i: (bi, i, 0))],
        out_shape=[head_shape] * 6 + [dense_shape] * 3
                  + [jax.ShapeDtypeStruct((b, s, LANES), F32)],
        scratch_shapes=[pltpu.VMEM((1, LANES), F32)],
        compiler_params=_params("arbitrary", "arbitrary"),
        name="in_proj",
    )(x, g, w, qn, wuq, kvn, wukv, tcos, ts1, ts2, fb, tri)


def _sweep_kernel(*refs, decay, chunk_mask):
    if decay:
        (q_ref, k_ref, v_ref, cq_ref, ck_ref, o_ref,
         m_sc, l_sc, acc_sc, s_sc, p_sc, alpha_sc, kmax_sc, ckmin_sc) = refs
    else:
        q_ref, k_ref, v_ref, o_ref, m_sc, l_sc, acc_sc, s_sc, p_sc, alpha_sc = refs
    i = pl.program_id(2)
    t = ATT_TILE
    q = q_ref[0, 0]
    m_sc[...] = jnp.full_like(m_sc, NEG)
    l_sc[...] = jnp.zeros_like(l_sc)
    acc_sc[...] = jnp.zeros_like(acc_sc)
    if decay:
        nblk = ck_ref.shape[2]

        @pl.when(i == 0)
        def _():
            def key_norm(j, acc):
                kf = k_ref[0, 0, pl.ds(pl.multiple_of(j * t, t), t), :].astype(F32)
                return jnp.maximum(acc, jnp.max(jnp.sum(kf * kf, axis=-1, keepdims=True),
                                                axis=0, keepdims=True))
            kmax_sc[...] = jnp.sqrt(lax.fori_loop(0, nblk, key_norm, jnp.zeros((1, 1), F32)))
            ckmin_sc[...] = jnp.min(ck_ref[0, 0], axis=-1, keepdims=True)

        head_lane = _lane_iota((t, LANES)) == pl.program_id(1)
        cq = jnp.sum(jnp.where(head_lane, cq_ref[0], 0.0), axis=-1, keepdims=True)
        cq_rep = jnp.broadcast_to(cq, (t, LANES))

    def first_needed_block():
        if not decay:
            return 0
        qf = q.astype(F32)
        kd = k_ref[0, 0, pl.ds(pl.multiple_of(i * t, t), t), :].astype(F32)
        gap = (jnp.sqrt(jnp.sum(qf * qf, axis=-1, keepdims=True)) * kmax_sc[...]
               - jnp.sum(qf * kd, axis=-1, keepdims=True) + cq)
        gap_max = jnp.max(gap, axis=0, keepdims=True)
        needed = gap_max - ckmin_sc[...] >= -SKIP_LOG2
        blk = lax.broadcasted_iota(jnp.int32, (nblk, 1), 0)
        return jnp.min(jnp.where(needed, blk, i))

    def scores(j, slot):
        off = pl.multiple_of(j * t, t)
        k = k_ref[0, 0, pl.ds(off, t), :]
        s_sc[slot] = lax.dot_general(q, k, NT_DIMS, preferred_element_type=F32)

    def softmax(j, slot, masked):
        s = s_sc[slot]
        if decay:
            s = s - ck_ref[0, 0, pl.ds(j, 1), :]
        if masked:
            qi = lax.broadcasted_iota(jnp.int32, (t, t), 0)
            ki = lax.broadcasted_iota(jnp.int32, (t, t), 1)
            if chunk_mask:
                qi, ki = qi // CHUNK, ki // CHUNK
            s = jnp.where(ki <= qi, s, NEG)
        m_old = m_sc[...]
        m_cur = jnp.max(s, axis=-1, keepdims=True)
        if decay:
            m_cur = m_cur + cq_rep
        m_new = jnp.maximum(m_old, m_cur)
        alpha = jnp.exp2(m_old - m_new)
        shift = m_new - cq_rep if decay else m_new
        ps = [jnp.exp2(s[:, c * LANES:(c + 1) * LANES] - shift) for c in range(t // LANES)]
        if not decay:
            l_sc[...] = alpha * l_sc[...] + functools.reduce(lambda a, b: a + b, ps)
        p_sc[slot] = jnp.concatenate(ps, axis=-1).astype(BF16)
        alpha_sc[slot] = alpha
        m_sc[...] = m_new

    def weigh(j, slot):
        off = pl.multiple_of(j * t, t)
        v = v_ref[0, 0, pl.ds(off, t), :]
        acc_sc[...] = alpha_sc[slot] * acc_sc[...] + jnp.dot(p_sc[slot], v,
                                                              preferred_element_type=F32)

    scores(i, 0)
    first = first_needed_block()
    n_full = i - first
    scores(first, 1)
    softmax(i, 0, True)
    unroll = SWEEP_UNROLL
    trips = jnp.maximum(n_full - 1, 0) // unroll

    def body(tt, carry):
        j = first + unroll * tt
        weigh(jnp.where(tt == 0, i, j - 1), 0)
        for u in range(unroll):
            scores(j + u + 1, u % 2)
            softmax(j + u, (u + 1) % 2, False)
            if u + 1 < unroll:
                weigh(j + u, (u + 1) % 2)
        return carry

    lax.fori_loop(0, trips, body, 0)
    left = n_full - unroll * trips
    j_left = first + unroll * trips
    j_pending = jnp.where(trips == 0, i, j_left - 1)

    def tail(count):
        weigh(j_pending, 0)
        for u in range(count):
            if u + 1 < count:
                scores(j_left + u + 1, u % 2)
            softmax(j_left + u, (u + 1) % 2, False)
            weigh(j_left + u, (u + 1) % 2)

    for count in range(unroll + 1):
        pl.when(left == count)(functools.partial(tail, count))

    acc = acc_sc[...]
    if decay:
        denom = acc[:, FOX_DIM:FOX_DIM + 1]
    else:
        denom = jnp.sum(l_sc[...], axis=-1, keepdims=True)
    o_ref[0, 0] = acc / denom


def _sweep(q, k, v, cum=None, ck=None, *, chunk_mask):
    b, nh, s, _ = q.shape
    t = ATT_TILE
    decay = cum is not None
    q_spec = pl.BlockSpec((1, 1, t, LANES), lambda bi, hi, i: (bi, hi, i, 0))
    kv_spec = pl.BlockSpec((1, 1, s, LANES), lambda bi, hi, i: (bi, hi, 0, 0))
    in_specs = [q_spec, kv_spec, kv_spec]
    args = [q, k, v]
    scratch = [pltpu.VMEM((t, LANES), F32), pltpu.VMEM((t, LANES), F32),
               pltpu.VMEM((t, LANES), F32), pltpu.VMEM((2, t, t), F32),
               pltpu.VMEM((2, t, t), BF16), pltpu.VMEM((2, t, LANES), F32)]
    if decay:
        in_specs += [pl.BlockSpec((1, t, LANES), lambda bi, hi, i: (bi, i, 0)),
                     pl.BlockSpec((1, 1, s // t, t), lambda bi, hi, i: (bi, hi, 0, 0))]
        args += [cum, ck]
        scratch += [pltpu.VMEM((1, 1), F32), pltpu.VMEM((s // t, 1), F32)]
    return pl.pallas_call(
        functools.partial(_sweep_kernel, decay=decay, chunk_mask=chunk_mask),
        grid=(b, nh, s // t),
        in_specs=in_specs,
        out_specs=q_spec,
        out_shape=jax.ShapeDtypeStruct((b, nh, s, LANES), F32),
        scratch_shapes=scratch,
        compiler_params=_params("arbitrary", "arbitrary", "arbitrary"),
        name="fox_sweep" if decay else "mla_sweep",
    )(*args)


def _rel_bias_kernel(tab_ref, o_ref):
    qi = lax.broadcasted_iota(jnp.int32, (CHUNK, BAND), 0)
    ki = lax.broadcasted_iota(jnp.int32, (CHUNK, BAND), 1)
    idx = jnp.clip(qi + CHK_LEFT * CHUNK - ki, -(CHUNK - 1), REL_MAX) + (CHUNK - 1)
    for hd in range(CHK_HEADS):
        def body(r, acc):
            return jnp.where(idx == r, tab_ref[hd, r], acc)
        o_ref[hd] = lax.fori_loop(0, REL_SIZE, body, jnp.zeros((CHUNK, BAND), F32)) * LOG2E


def _rel_bias(table):
    return pl.pallas_call(
        _rel_bias_kernel,
        in_specs=[pl.BlockSpec(memory_space=pltpu.SMEM)],
        out_specs=pl.BlockSpec(memory_space=pltpu.VMEM),
        out_shape=jax.ShapeDtypeStruct((CHK_HEADS, CHUNK, BAND), F32),
        name="rel_bias",
    )(table)


def _chunk_kernel(q_ref, k_ref, v_ref, bias_ref, o_ref):
    lane = _lane_iota((CHK_TILE, C_WIDTH)) // CHK_DIM
    lead = CHK_LEFT * CHUNK // CHK_TILE
    tiles = CHK_STEP // CHK_TILE
    units = []
    for u in range(tiles):
        tile = pl.program_id(1) * tiles + u
        start = pl.multiple_of(jnp.maximum(tile - lead, 0) * CHK_TILE, CHK_TILE)
        shift = jnp.maximum(lead - tile, 0)
        units += [(u, hd, start, shift) for hd in range(CHK_HEADS)]

    def scores(unit):
        u, hd, start, _ = unit
        q = q_ref[0, u * CHK_TILE:(u + 1) * CHK_TILE, :]
        kw = k_ref[0, pl.ds(start, CHK_WIN), :]
        return lax.dot_general(jnp.where(lane == hd, q, jnp.zeros((), q.dtype)), kw, NT_DIMS,
                               preferred_element_type=F32)

    def attend(unit, s):
        _, hd, start, shift = unit
        s = s + bias_ref[shift, hd]
        m = jnp.max(s, axis=-1, keepdims=True)
        p = jnp.exp2(s - m)
        denom = jnp.sum(p, axis=-1, keepdims=True)
        vw = v_ref[0, pl.ds(start, CHK_WIN), :]
        full = jnp.dot(p.astype(BF16), vw, preferred_element_type=F32)
        return jnp.where(lane == hd, full / denom, 0.0)

    outs = [None] * tiles
    s_next = scores(units[0])
    for n, unit in enumerate(units):
        u, hd = unit[:2]
        s_cur = s_next
        if n + 1 < len(units):
            s_next = scores(units[n + 1])
        o = attend(unit, s_cur)
        outs[u] = o if outs[u] is None else outs[u] + o
        if hd == CHK_HEADS - 1:
            o_ref[0, u * CHK_TILE:(u + 1) * CHK_TILE, :] = outs[u]


def _chunk_attn(q, k, v, bias):
    b, s, _ = q.shape
    return pl.pallas_call(
        _chunk_kernel,
        grid=(b, s // CHK_STEP),
        in_specs=[pl.BlockSpec((1, CHK_STEP, C_WIDTH), lambda bi, i: (bi, i, 0)),
                  pl.BlockSpec((1, s, C_WIDTH), lambda bi, i: (bi, 0, 0)),
                  pl.BlockSpec((1, s, C_WIDTH), lambda bi, i: (bi, 0, 0)),
                  pl.BlockSpec(bias.shape, lambda bi, i: (0, 0, 0, 0),
                               pipeline_mode=pl.Buffered(1))],
        out_specs=pl.BlockSpec((1, CHK_STEP, C_WIDTH), lambda bi, i: (bi, i, 0)),
        out_shape=jax.ShapeDtypeStruct((b, s, C_WIDTH), F32),
        compiler_params=_params("arbitrary", "arbitrary"),
        name="chunk_attn",
    )(q, k, v, bias)


def _mem_kernel(m_ref, g_ref, w_ref, k_ref, v_ref):
    h = _rms(m_ref[0], g_ref[...], D_MODEL).astype(BF16)
    kv = jnp.dot(h, w_ref[...], preferred_element_type=F32)
    k_ref[0] = kv[:, :CROSS_WIDTH].astype(BF16)
    v_ref[0] = kv[:, CROSS_WIDTH:].astype(BF16)


def _mem_kv(mem, g, w, layer):
    b = mem.shape[0]
    spec = pl.BlockSpec((1, MEM_LEN, CROSS_WIDTH), lambda bi: (bi, 0, 0))
    shape = jax.ShapeDtypeStruct((b, MEM_LEN, CROSS_WIDTH), BF16)
    return pl.pallas_call(
        _mem_kernel,
        grid=(b,),
        in_specs=[pl.BlockSpec((1, MEM_LEN, D_MODEL), lambda bi: (bi, 0, 0)),
                  _const_spec((1, D_MODEL)), _layer_spec((D_MODEL, 2 * CROSS_WIDTH), layer)],
        out_specs=[spec, spec],
        out_shape=[shape, shape],
        compiler_params=_params("arbitrary"),
        name="mem_kv",
    )(mem, g, w)


def _mix_cross_kernel(x_ref, ya_ref, yb_ref, yc_ref, ga_ref, gb_ref, gc_ref, woa_ref, wob_ref,
                      woc_ref, gx_ref, wcq_ref, km_ref, vm_ref, wco_ref, o_ref):
    tm = x_ref.shape[1]
    th = tm // MIX_SPLIT
    lane = _lane_iota((th, LANES))
    heads = [slice(hd * CROSS_DIM, (hd + 1) * CROSS_DIM) for hd in range(CROSS_HEADS)]
    subs = [dict(rows=slice(r * th, (r + 1) * th)) for r in range(MIX_SPLIT)]

    def stage_mix(st):
        rows = st["rows"]
        ya = jnp.concatenate([ya_ref[0, hd, rows, :] for hd in range(MLA_HEADS)], axis=-1)
        yb = jnp.concatenate([jnp.where(lane < FOX_DIM, yb_ref[0, hd, rows, :], 0.0)
                              for hd in range(FOX_HEADS)], axis=-1)
        st["x1"] = (x_ref[0, rows, :]
                    + jnp.dot(_rms(ya, ga_ref[...], A_WIDTH).astype(BF16), woa_ref[...],
                              preferred_element_type=F32)
                    + jnp.dot(_rms(yb, gb_ref[...], B_WIDTH).astype(BF16), wob_ref[...],
                              preferred_element_type=F32)
                    + jnp.dot(_rms(yc_ref[0, rows, :], gc_ref[...], C_WIDTH).astype(BF16),
                              woc_ref[...], preferred_element_type=F32))

    def stage_query(st):
        h = _rms(st["x1"], gx_ref[...], D_MODEL).astype(BF16)
        st["q"] = (jnp.dot(h, wcq_ref[...], preferred_element_type=F32)
                   * (CROSS_DIM ** -0.5 * LOG2E)).astype(BF16)

    def stage_scores(st):
        q = st.pop("q")
        st["s"] = [lax.dot_general(q[:, sl], km_ref[0, :, sl], NT_DIMS,
                                   preferred_element_type=F32) for sl in heads]

    def stage_attend(st):
        outs = []
        for s, sl in zip(st.pop("s"), heads):
            m = jnp.max(s, axis=-1, keepdims=True)
            p = jnp.exp2(s - m)
            denom = jnp.sum(p, axis=-1, keepdims=True)
            outs.append(jnp.dot(p.astype(BF16), vm_ref[0, :, sl], preferred_element_type=F32)
                        / denom)
        st["o"] = jnp.concatenate(outs, axis=-1).astype(BF16)

    def stage_out(st):
        o_ref[0, st["rows"], :] = st.pop("x1") + jnp.dot(st.pop("o"), wco_ref[...],
                                                         preferred_element_type=F32)

    for stage in (stage_mix, stage_query, stage_scores, stage_attend, stage_out):
        for st in subs:
            stage(st)


def _mix_cross(x, ya, yb, yc, ga, gb, gc, wo, wob, gx, wcq, km, vm, wco, layer):
    b, s, _ = x.shape
    tm = TOK_TILE
    x_spec = pl.BlockSpec((1, tm, D_MODEL), lambda bi, i: (bi, i, 0))
    head_spec = pl.BlockSpec((1, 4, tm, LANES), lambda bi, i: (bi, 0, i, 0))
    mem_spec = pl.BlockSpec((1, MEM_LEN, CROSS_WIDTH), lambda bi, i: (bi, 0, 0))
    return pl.pallas_call(
        _mix_cross_kernel,
        grid=(b, s // tm),
        in_specs=[x_spec, head_spec, head_spec,
                  pl.BlockSpec((1, tm, C_WIDTH), lambda bi, i: (bi, i, 0)),
                  _const_spec((1, A_WIDTH)), _const_spec((1, 512)), _const_spec((1, C_WIDTH)),
                  _layer_spec((A_WIDTH, D_MODEL), layer), _const_spec((512, D_MODEL)),
                  _layer_spec((C_WIDTH, D_MODEL), layer, (A_WIDTH + B_WIDTH) // C_WIDTH),
                  _const_spec((1, D_MODEL)), _layer_spec((D_MODEL, CROSS_WIDTH), layer),
                  mem_spec, mem_spec, _layer_spec((CROSS_WIDTH, D_MODEL), layer)],
        out_specs=x_spec,
        out_shape=jax.ShapeDtypeStruct(x.shape, F32),
        compiler_params=_params("arbitrary", "arbitrary"),
        name="mix_cross",
    )(x, ya, yb, yc, ga, gb, gc, wo, wob, wo, gx, wcq, km, vm, wco)


def _ffn_kernel(x_ref, g_ref, wgu_ref, wd_ref, gf_ref, o_ref, acc_ref, *, final_norm):
    x = x_ref[0]
    h = _rms(x, g_ref[...], D_MODEL).astype(BF16)
    acc_ref[...] = x
    nblk = FFN_HIDDEN // FFN_BLOCK

    def gate_up(c):
        lo = c * FFN_BLOCK
        gate = jnp.dot(h, wgu_ref[:, lo:lo + FFN_BLOCK], preferred_element_type=F32)
        up = jnp.dot(h, wgu_ref[:, FFN_HIDDEN + lo:FFN_HIDDEN + lo + FFN_BLOCK],
                     preferred_element_type=F32)
        return gate, up

    nxt = gate_up(0)
    for c in range(nblk):
        gate, up = nxt
        if c + 1 < nblk:
            nxt = gate_up(c + 1)
        act = (gate * jax.nn.sigmoid(gate) * up).astype(BF16)
        acc_ref[...] += jnp.dot(act, wd_ref[c * FFN_BLOCK:(c + 1) * FFN_BLOCK, :],
                                preferred_element_type=F32)
    y = acc_ref[...]
    o_ref[0] = _rms(y, gf_ref[...], D_MODEL) if final_norm else y


def _ffn(x, g, wgu, wd, gf, layer, *, final_norm):
    b, s, _ = x.shape
    tm = TOK_TILE
    x_spec = pl.BlockSpec((1, tm, D_MODEL), lambda bi, i: (bi, i, 0))

    def resident(shape):
        return _layer_spec(shape, layer, pipeline_mode=pl.Buffered(1))

    return pl.pallas_call(
        functools.partial(_ffn_kernel, final_norm=final_norm),
        grid=(b, s // tm),
        in_specs=[x_spec, _const_spec((1, D_MODEL)),
                  resident((D_MODEL, 2 * FFN_HIDDEN)), resident((FFN_HIDDEN, D_MODEL)),
                  _const_spec((1, D_MODEL))],
        out_specs=x_spec,
        out_shape=jax.ShapeDtypeStruct(x.shape, F32),
        scratch_shapes=[pltpu.VMEM((tm, D_MODEL), F32)],
        compiler_params=_params("arbitrary", "arbitrary"),
        name="ffn",
    )(x, g, wgu, wd, gf)


def _pad_cols(a, width):
    return jnp.pad(a, ((0, 0), (0, width - a.shape[1])))


def _layer_weights(w_in, w_uq, w_ukv, w_o, out_norm, f_bias):
    sizes = (256, 128, 32, 256, 256, 256, 4, 256, 256, 256)
    parts, at = [], 0
    for n in sizes:
        parts.append(w_in[:, at:at + n])
        at += n
    c_q, c_kv, k_rope, fq, fk, fv, f_logit, cq, ck, cv = parts
    zeros = lambda n: jnp.zeros((D_MODEL, n), F32)
    misc = jnp.concatenate([f_logit, zeros(MLA_NOPE - FOX_HEADS), k_rope, zeros(32)], axis=1)
    w_in_r = jnp.concatenate([c_q, c_kv, misc, fq, fk, fv, cq, ck, cv], axis=1).astype(BF16)
    wuq = jnp.pad(w_uq.reshape(MLA_Q_RANK, MLA_HEADS, MLA_NOPE + MLA_ROPE),
                  ((0, 0), (0, 0), (0, 32))).reshape(MLA_Q_RANK, 512).astype(BF16)
    wukv = w_ukv.reshape(MLA_KV_RANK, MLA_HEADS, MLA_NOPE + MLA_V)
    wk = jnp.pad(wukv[:, :, :MLA_NOPE], ((0, 0), (0, 0), (0, 64))).reshape(MLA_KV_RANK, 512)
    wv = wukv[:, :, MLA_NOPE:].reshape(MLA_KV_RANK, 512)
    wukv_r = jnp.concatenate([wk, wv], axis=1).astype(BF16)
    wob = jnp.pad(w_o[A_WIDTH:A_WIDTH + B_WIDTH].reshape(FOX_HEADS, FOX_DIM, D_MODEL),
                  ((0, 0), (0, 64), (0, 0))).reshape(512, D_MODEL).astype(BF16)
    ga = out_norm[:A_WIDTH].reshape(1, A_WIDTH)
    gb = jnp.pad(out_norm[A_WIDTH:A_WIDTH + B_WIDTH].reshape(FOX_HEADS, FOX_DIM),
                 ((0, 0), (0, 64))).reshape(1, 512)
    gc = out_norm[A_WIDTH + B_WIDTH:].reshape(1, C_WIDTH)
    fb = _pad_cols(f_bias.reshape(1, FOX_HEADS), LANES)
    return w_in_r, wuq, wukv_r, wob, ga, gb, gc, fb


def _rope_tables(seq):
    pos = jnp.arange(seq, dtype=F32)
    inv = ROPE_THETA ** (-jnp.arange(0, MLA_ROPE, 2, dtype=F32) / MLA_ROPE)
    ang = pos[:, None] * inv[None, :]
    cos, sin = jnp.cos(ang), jnp.sin(ang)
    z = lambda n: jnp.zeros((seq, n), F32)
    half = MLA_ROPE // 2
    tcos = jnp.concatenate([z(MLA_NOPE), cos, cos, z(32)], axis=1)
    ts1 = jnp.concatenate([z(MLA_NOPE), -sin, z(half), z(32)], axis=1)
    ts2 = jnp.concatenate([z(MLA_NOPE), z(half), sin, z(32)], axis=1)
    return tcos, ts1, ts2


def _tile_bias(bias):
    nq = CHK_TILE // CHUNK
    rows = [jnp.pad(bias, ((0, 0), (0, 0), (c * CHUNK, CHK_WIN - BAND - c * CHUNK)),
                    constant_values=NEG) for c in range(nq)]
    tile = jnp.concatenate(rows, axis=1)
    lead = CHK_LEFT * CHUNK // CHK_TILE
    return jnp.stack([jnp.pad(tile[:, :, v * CHK_TILE:], ((0, 0), (0, 0), (0, v * CHK_TILE)),
                              constant_values=NEG) for v in range(lead + 1)])


def kernel(x, mem, norm_mix, w_in, q_norm, w_uq, kv_norm, w_ukv, f_bias, rel_bias, out_norm, w_o,
           norm_cross, norm_mem, w_cq, w_ckv, w_co, norm_ffn, w_gu, w_down, final_norm):
    b, s, _ = x.shape
    depth = w_in.shape[0]
    tcos, ts1, ts2 = _rope_tables(s)
    tri = jnp.tril(jnp.ones((TOK_TILE, TOK_TILE), BF16))
    row = lambda v: v.reshape(1, -1)
    wo_b, wcq_b, wckv_b, wco_b, wgu_b, wd_b = (w.astype(BF16)
                                                for w in (w_o, w_cq, w_ckv, w_co, w_gu, w_down))
    for l in range(depth):
        (w_in_r, wuq, wukv_r, wob, ga, gb, gc, fb) = _layer_weights(
            w_in[l], w_uq[l], w_ukv[l], w_o[l], out_norm[l], f_bias[l])
        mq, mk, mv, fq, fk, fv, cq, ck, cv, cum = _in_proj(
            x, row(norm_mix[l]), w_in_r, row(q_norm[l]), wuq, row(kv_norm[l]), wukv_r,
            tcos, ts1, ts2, fb, tri)
        ya = _sweep(mq, mk, mv, chunk_mask=True)
        cum_h = jnp.transpose(cum[:, :, :FOX_HEADS], (0, 2, 1))
        yb = _sweep(fq, fk, fv, cum, cum_h.reshape(b, FOX_HEADS, s // ATT_TILE, ATT_TILE),
                    chunk_mask=False)
        yc = _chunk_attn(cq, ck, cv, _tile_bias(_rel_bias(rel_bias[l])))
        km, vm = _mem_kv(mem, row(norm_mem[l]), wckv_b, l)
        x = _mix_cross(x, ya, yb, yc, ga, gb, gc, wo_b, wob, row(norm_cross[l]), wcq_b, km, vm,
                       wco_b, l)
        x = _ffn(x, row(norm_ffn[l]), wgu_b, wd_b, row(final_norm), l,
                 final_norm=(l == depth - 1))
    return x
```

```python
import functools

import jax
import jax.numpy as jnp
from jax import lax
from jax.experimental import pallas as pl
from jax.experimental.pallas import tpu as pltpu

D_MODEL = 1024
CHUNK = 64
MEM_LEN = 256
EPS = 1e-6
MLA_HEADS = 4
MLA_Q_RANK = 256
MLA_KV_RANK = 128
MLA_NOPE = 64
MLA_ROPE = 32
MLA_V = 128
ROPE_THETA = 10000.0
FOX_HEADS = 4
FOX_DIM = 64
CHK_HEADS = 4
CHK_DIM = 64
CHK_LEFT = 8
BAND = (CHK_LEFT + 1) * CHUNK
REL_MAX = 128
REL_SIZE = (CHUNK - 1) + REL_MAX + 1
A_WIDTH = MLA_HEADS * MLA_V
B_WIDTH = FOX_HEADS * FOX_DIM
C_WIDTH = CHK_HEADS * CHK_DIM
CROSS_HEADS = 4
CROSS_DIM = 128
CROSS_WIDTH = CROSS_HEADS * CROSS_DIM
FFN_HIDDEN = 2816

LANES = 128
LOG2E = 1.4426950408889634
NEG = -1e30
SKIP_LOG2 = 152.0
VMEM_LIMIT = 56 * 1024 * 1024

TOK_TILE = 512
ATT_TILE = 512
SWEEP_UNROLL = 4
MLA_GROUP = 2
CHK_TILE = 256
CHK_WIN = CHK_TILE + CHK_LEFT * CHUNK
CHK_STEP = 4 * CHK_TILE
FFN_BLOCK = 256
IN_COLS = 2048
IN_SPLIT = 2
MIX_SPLIT = 2

BF16 = jnp.bfloat16
F32 = jnp.float32
NT_DIMS = (((1,), (1,)), ((), ()))


def _params(*semantics):
    return pltpu.CompilerParams(dimension_semantics=semantics, vmem_limit_bytes=VMEM_LIMIT)


def _rms(xf, gain, width):
    return xf * lax.rsqrt(jnp.sum(xf * xf, axis=-1, keepdims=True) * (1.0 / width) + EPS) * gain


def _lane_iota(shape):
    return lax.broadcasted_iota(jnp.int32, shape, len(shape) - 1)


def _const_spec(shape):
    zeros = (0,) * len(shape)
    return pl.BlockSpec(shape, lambda *_: zeros)


def _layer_spec(shape, layer, block=0, **kwargs):
    index = (layer, block, 0)
    return pl.BlockSpec((None,) + tuple(shape), lambda *_: index, **kwargs)


def _in_proj_kernel(x_ref, g_ref, w_ref, qn_ref, wuq_ref, kvn_ref, wukv_ref, tc_ref, ts1_ref,
                    ts2_ref, fb_ref, tri_ref,
                    mq_ref, mk_ref, mv_ref, fq_ref, fk_ref, fv_ref, cq_ref, ck_ref, cv_ref,
                    cum_ref, carry_ref):
    i = pl.program_id(1)
    tm = x_ref.shape[1]
    th = tm // IN_SPLIT
    lane = _lane_iota((th, LANES))
    q_scale = (MLA_NOPE + MLA_ROPE) ** -0.5 * LOG2E
    f_scale = FOX_DIM ** -0.5 * LOG2E
    ones_col = jnp.where(lane == FOX_DIM, 1.0, 0.0)

    @pl.when(i == 0)
    def _():
        carry_ref[...] = jnp.zeros_like(carry_ref)

    subs = [dict(rows=slice(r * th, (r + 1) * th)) for r in range(IN_SPLIT)]

    def proj(st, lo, hi):
        return jnp.dot(st["h"], w_ref[:, lo:hi], preferred_element_type=F32)

    def rope(st, y, keep_low):
        rows = st["rows"]
        tcos = tc_ref[rows, :]
        base = tcos + jnp.where(lane < MLA_NOPE, 1.0, 0.0) if keep_low else tcos
        return (y * base + pltpu.roll(y, LANES - MLA_ROPE // 2, 1) * ts1_ref[rows, :]
                + pltpu.roll(y, MLA_ROPE // 2, 1) * ts2_ref[rows, :])

    def head_block(y, hd):
        blk = y[:, (hd // 2) * LANES:(hd // 2 + 1) * LANES]
        return pltpu.roll(blk, FOX_DIM, 1) if hd % 2 else blk

    def stage_norm(st):
        st["h"] = _rms(x_ref[0, st["rows"], :], g_ref[...], D_MODEL).astype(BF16)

    def stage_latent(st):
        st["c_q"] = proj(st, 0, 256)
        st["misc"] = proj(st, 384, 512)
        st["c_kv"] = proj(st, 256, 384)

    def stage_up(st):
        st["q_up"] = jnp.dot(_rms(st.pop("c_q"), qn_ref[...], MLA_Q_RANK).astype(BF16),
                             wuq_ref[...], preferred_element_type=F32)
        st["kv_up"] = jnp.dot(_rms(st.pop("c_kv"), kvn_ref[...], MLA_KV_RANK).astype(BF16),
                              wukv_ref[...], preferred_element_type=F32)

    def stage_fox_proj(st):
        st["fq"] = proj(st, 512, 768)
        st["fk"] = proj(st, 768, 1024)
        st["fv"] = proj(st, 1024, 1280)

    def stage_mla_store(st):
        rows = st["rows"]
        q_up, kv_up = st.pop("q_up"), st.pop("kv_up")
        k_pe = rope(st, st["misc"], False)
        for hd in range(MLA_HEADS):
            sl = slice(hd * LANES, (hd + 1) * LANES)
            mq_ref[0, hd, rows, :] = (rope(st, q_up[:, sl], True) * q_scale).astype(BF16)
            mk_ref[0, hd, rows, :] = (kv_up[:, sl] + k_pe).astype(BF16)
            mv_ref[0, hd, rows, :] = kv_up[:, A_WIDTH + hd * LANES:
                                           A_WIDTH + (hd + 1) * LANES].astype(BF16)

    def stage_chunk_proj(st):
        rows = st["rows"]
        cq_ref[0, rows, :] = (proj(st, 1280, 1536) * (CHK_DIM ** -0.5 * LOG2E)).astype(BF16)
        ck_ref[0, rows, :] = proj(st, 1536, 1792).astype(BF16)
        cv_ref[0, rows, :] = proj(st, 1792, 2048).astype(BF16)

    def stage_fox_store(st):
        rows = st["rows"]
        fq, fk, fv = st.pop("fq"), st.pop("fk"), st.pop("fv")
        for hd in range(FOX_HEADS):
            fq_ref[0, hd, rows, :] = jnp.where(lane < FOX_DIM, head_block(fq, hd) * f_scale,
                                               0.0).astype(BF16)
            fk_ref[0, hd, rows, :] = jnp.where(lane < FOX_DIM, head_block(fk, hd),
                                               0.0).astype(BF16)
            fv_ref[0, hd, rows, :] = jnp.where(lane < FOX_DIM, head_block(fv, hd),
                                               ones_col).astype(BF16)

    def stage_gate(st):
        z = st.pop("misc") + fb_ref[...]
        log_f = jnp.minimum(z, 0.0) - jnp.log1p(jnp.exp(-jnp.abs(z)))
        log_f = jnp.where(lane < FOX_HEADS, log_f, 0.0)
        p_hi = log_f.astype(BF16)
        rem = log_f - p_hi.astype(F32)
        p_mid = rem.astype(BF16)
        p_lo = (rem - p_mid.astype(F32)).astype(BF16)
        parts = jnp.dot(tri_ref[:th, :th], jnp.concatenate([p_hi, p_mid, p_lo], axis=-1),
                        preferred_element_type=F32)
        st["cum"] = parts[:, :LANES] + parts[:, LANES:2 * LANES] + parts[:, 2 * LANES:]

    for stage in (stage_norm, stage_latent, stage_up, stage_fox_proj, stage_mla_store,
                  stage_chunk_proj, stage_fox_store, stage_gate):
        for st in subs:
            stage(st)

    offset = carry_ref[...]
    for st in subs:
        cum = st["cum"] + offset
        offset = cum[th - 1:th, :]
        cum_ref[0, st["rows"], :] = cum * LOG2E
    carry_ref[...] = offset


def _in_proj(x, g, w, qn, wuq, kvn, wukv, tcos, ts1, ts2, fb, tri):
    b, s, _ = x.shape
    tm = TOK_TILE
    head_spec = pl.BlockSpec((1, 4, tm, LANES), lambda bi, i: (bi, 0, i, 0))
    dense_spec = pl.BlockSpec((1, tm, 256), lambda bi, i: (bi, i, 0))
    tab_spec = pl.BlockSpec((tm, LANES), lambda bi, i: (i, 0))
    head_shape = jax.ShapeDtypeStruct((b, 4, s, LANES), BF16)
    dense_shape = jax.ShapeDtypeStruct((b, s, 256), BF16)
    return pl.pallas_call(
        _in_proj_kernel,
        grid=(b, s // tm),
        in_specs=[pl.BlockSpec((1, tm, D_MODEL), lambda bi, i: (bi, i, 0)),
                  _const_spec((1, D_MODEL)), _const_spec((D_MODEL, IN_COLS)),
                  _const_spec((1, MLA_Q_RANK)), _const_spec((MLA_Q_RANK, 512)),
                  _const_spec((1, MLA_KV_RANK)), _const_spec((MLA_KV_RANK, 1024)),
                  tab_spec, tab_spec, tab_spec, _const_spec((1, LANES)), _const_spec((tm, tm))],
        out_specs=[head_spec] * 6 + [dense_spec] * 3
                  + [pl.BlockSpec((1, tm, LANES), lambda bi, i: (bi, i, 0))],
        out_shape=[head_shape] * 6 + [dense_shape] * 3
                  + [jax.ShapeDtypeStruct((b, s, LANES), F32)],
        scratch_shapes=[pltpu.VMEM((1, LANES), F32)],
        compiler_params=_params("arbitrary", "arbitrary"),
        name="in_proj",
    )(x, g, w, qn, wuq, kvn, wukv, tcos, ts1, ts2, fb, tri)


def _sweep_kernel(*refs, decay, chunk_mask, group):
    if decay:
        (q_ref, k_ref, v_ref, cq_ref, ck_ref, o_ref,
         m_sc, l_sc, acc_sc, s_sc, p_sc, alpha_sc, kmax_sc, ckmin_sc) = refs
    else:
        q_ref, k_ref, v_ref, o_ref, m_sc, l_sc, acc_sc, s_sc, p_sc, alpha_sc = refs
    i = pl.program_id(2)
    t = ATT_TILE
    heads = range(group)
    m_sc[...] = jnp.full_like(m_sc, NEG)
    l_sc[...] = jnp.zeros_like(l_sc)
    acc_sc[...] = jnp.zeros_like(acc_sc)
    if decay:
        nblk = ck_ref.shape[2]

        @pl.when(i == 0)
        def _():
            def key_norm(j, acc):
                kf = k_ref[0, 0, pl.ds(pl.multiple_of(j * t, t), t), :].astype(F32)
                return jnp.maximum(acc, jnp.max(jnp.sum(kf * kf, axis=-1, keepdims=True),
                                                axis=0, keepdims=True))
            kmax_sc[...] = jnp.sqrt(lax.fori_loop(0, nblk, key_norm, jnp.zeros((1, 1), F32)))
            ckmin_sc[...] = jnp.min(ck_ref[0, 0], axis=-1, keepdims=True)

        head_lane = _lane_iota((t, LANES)) == pl.program_id(1)
        cq = jnp.sum(jnp.where(head_lane, cq_ref[0], 0.0), axis=-1, keepdims=True)
        cq_rep = jnp.broadcast_to(cq, (t, LANES))

    def first_needed_block():
        if not decay:
            return 0
        qf = q_ref[0, 0].astype(F32)
        kd = k_ref[0, 0, pl.ds(pl.multiple_of(i * t, t), t), :].astype(F32)
        gap = (jnp.sqrt(jnp.sum(qf * qf, axis=-1, keepdims=True)) * kmax_sc[...]
               - jnp.sum(qf * kd, axis=-1, keepdims=True) + cq)
        gap_max = jnp.max(gap, axis=0, keepdims=True)
        needed = gap_max - ckmin_sc[...] >= -SKIP_LOG2
        blk = lax.broadcasted_iota(jnp.int32, (nblk, 1), 0)
        return jnp.min(jnp.where(needed, blk, i))

    def scores(j, slot):
        off = pl.multiple_of(j * t, t)
        for g in heads:
            k = k_ref[0, g, pl.ds(off, t), :]
            s_sc[g, slot] = lax.dot_general(q_ref[0, g], k, NT_DIMS, preferred_element_type=F32)

    def softmax(j, slot, masked):
        for g in heads:
            s = s_sc[g, slot]
            if decay:
                s = s - ck_ref[0, 0, pl.ds(j, 1), :]
            if masked:
                qi = lax.broadcasted_iota(jnp.int32, (t, t), 0)
                ki = lax.broadcasted_iota(jnp.int32, (t, t), 1)
                if chunk_mask:
                    qi, ki = qi // CHUNK, ki // CHUNK
                s = jnp.where(ki <= qi, s, NEG)
            m_old = m_sc[g]
            m_cur = jnp.max(s, axis=-1, keepdims=True)
            if decay:
                m_cur = m_cur + cq_rep
            m_new = jnp.maximum(m_old, m_cur)
            alpha = jnp.exp2(m_old - m_new)
            shift = m_new - cq_rep if decay else m_new
            ps = [jnp.exp2(s[:, c * LANES:(c + 1) * LANES] - shift) for c in range(t // LANES)]
            if not decay:
                l_sc[g] = alpha * l_sc[g] + functools.reduce(lambda a, b: a + b, ps)
            p_sc[g, slot] = jnp.concatenate(ps, axis=-1).astype(BF16)
            alpha_sc[g, slot] = alpha
            m_sc[g] = m_new

    def weigh(j, slot):
        off = pl.multiple_of(j * t, t)
        for g in heads:
            v = v_ref[0, g, pl.ds(off, t), :]
            acc_sc[g] = alpha_sc[g, slot] * acc_sc[g] + jnp.dot(p_sc[g, slot], v,
                                                                preferred_element_type=F32)

    scores(i, 0)
    first = first_needed_block()
    n_full = i - first
    scores(first, 1)
    softmax(i, 0, True)
    unroll = SWEEP_UNROLL
    trips = jnp.maximum(n_full - 1, 0) // unroll

    def body(tt, carry):
        j = first + unroll * tt
        weigh(jnp.where(tt == 0, i, j - 1), 0)
        for u in range(unroll):
            scores(j + u + 1, u % 2)
            softmax(j + u, (u + 1) % 2, False)
            if u + 1 < unroll:
                weigh(j + u, (u + 1) % 2)
        return carry

    lax.fori_loop(0, trips, body, 0)
    left = n_full - unroll * trips
    j_left = first + unroll * trips
    j_pending = jnp.where(trips == 0, i, j_left - 1)

    def tail(count):
        weigh(j_pending, 0)
        for u in range(count):
            if u + 1 < count:
                scores(j_left + u + 1, u % 2)
            softmax(j_left + u, (u + 1) % 2, False)
            weigh(j_left + u, (u + 1) % 2)

    for count in range(unroll + 1):
        pl.when(left == count)(functools.partial(tail, count))

    for g in heads:
        acc = acc_sc[g]
        if decay:
            denom = acc[:, FOX_DIM:FOX_DIM + 1]
        else:
            denom = jnp.sum(l_sc[g], axis=-1, keepdims=True)
        o_ref[0, g] = acc / denom


def _sweep(q, k, v, cum=None, ck=None, *, chunk_mask):
    b, nh, s, _ = q.shape
    t = ATT_TILE
    decay = cum is not None
    group = 1 if decay else MLA_GROUP
    q_spec = pl.BlockSpec((1, group, t, LANES), lambda bi, hi, i: (bi, hi, i, 0))
    kv_spec = pl.BlockSpec((1, group, s, LANES), lambda bi, hi, i: (bi, hi, 0, 0))
    in_specs = [q_spec, kv_spec, kv_spec]
    args = [q, k, v]
    scratch = [pltpu.VMEM((group, t, LANES), F32), pltpu.VMEM((group, t, LANES), F32),
               pltpu.VMEM((group, t, LANES), F32), pltpu.VMEM((group, 2, t, t), F32),
               pltpu.VMEM((group, 2, t, t), BF16), pltpu.VMEM((group, 2, t, LANES), F32)]
    if decay:
        in_specs += [pl.BlockSpec((1, t, LANES), lambda bi, hi, i: (bi, i, 0)),
                     pl.BlockSpec((1, 1, s // t, t), lambda bi, hi, i: (bi, hi, 0, 0))]
        args += [cum, ck]
        scratch += [pltpu.VMEM((1, 1), F32), pltpu.VMEM((s // t, 1), F32)]
    return pl.pallas_call(
        functools.partial(_sweep_kernel, decay=decay, chunk_mask=chunk_mask, group=group),
        grid=(b, nh // group, s // t),
        in_specs=in_specs,
        out_specs=q_spec,
        out_shape=jax.ShapeDtypeStruct((b, nh, s, LANES), F32),
        scratch_shapes=scratch,
        compiler_params=_params("arbitrary", "arbitrary", "arbitrary"),
        name="fox_sweep" if decay else "mla_sweep",
    )(*args)


def _rel_bias_kernel(tab_ref, o_ref):
    qi = lax.broadcasted_iota(jnp.int32, (CHUNK, BAND), 0)
    ki = lax.broadcasted_iota(jnp.int32, (CHUNK, BAND), 1)
    idx = jnp.clip(qi + CHK_LEFT * CHUNK - ki, -(CHUNK - 1), REL_MAX) + (CHUNK - 1)
    for hd in range(CHK_HEADS):
        def body(r, acc):
            return jnp.where(idx == r, tab_ref[hd, r], acc)
        o_ref[hd] = lax.fori_loop(0, REL_SIZE, body, jnp.zeros((CHUNK, BAND), F32)) * LOG2E


def _rel_bias(table):
    return pl.pallas_call(
        _rel_bias_kernel,
        in_specs=[pl.BlockSpec(memory_space=pltpu.SMEM)],
        out_specs=pl.BlockSpec(memory_space=pltpu.VMEM),
        out_shape=jax.ShapeDtypeStruct((CHK_HEADS, CHUNK, BAND), F32),
        name="rel_bias",
    )(table)


def _chunk_kernel(q_ref, k_ref, v_ref, bias_ref, o_ref):
    lane = _lane_iota((CHK_TILE, C_WIDTH)) // CHK_DIM
    lead = CHK_LEFT * CHUNK // CHK_TILE
    tiles = CHK_STEP // CHK_TILE
    units = []
    for u in range(tiles):
        tile = pl.program_id(1) * tiles + u
        start = pl.multiple_of(jnp.maximum(tile - lead, 0) * CHK_TILE, CHK_TILE)
        shift = jnp.maximum(lead - tile, 0)
        units += [(u, hd, start, shift) for hd in range(CHK_HEADS)]

    def scores(unit):
        u, hd, start, _ = unit
        q = q_ref[0, u * CHK_TILE:(u + 1) * CHK_TILE, :]
        kw = k_ref[0, pl.ds(start, CHK_WIN), :]
        return lax.dot_general(jnp.where(lane == hd, q, jnp.zeros((), q.dtype)), kw, NT_DIMS,
                               preferred_element_type=F32)

    def attend(unit, s):
        _, hd, start, shift = unit
        s = s + bias_ref[shift, hd]
        m = jnp.max(s, axis=-1, keepdims=True)
        p = jnp.exp2(s - m)
        denom = jnp.sum(p, axis=-1, keepdims=True)
        vw = v_ref[0, pl.ds(start, CHK_WIN), :]
        full = jnp.dot(p.astype(BF16), vw, preferred_element_type=F32)
        return jnp.where(lane == hd, full / denom, 0.0)

    outs = [None] * tiles
    s_next = scores(units[0])
    for n, unit in enumerate(units):
        u, hd = unit[:2]
        s_cur = s_next
        if n + 1 < len(units):
            s_next = scores(units[n + 1])
        o = attend(unit, s_cur)
        outs[u] = o if outs[u] is None else outs[u] + o
        if hd == CHK_HEADS - 1:
            o_ref[0, u * CHK_TILE:(u + 1) * CHK_TILE, :] = outs[u]


def _chunk_attn(q, k, v, bias):
    b, s, _ = q.shape
    return pl.pallas_call(
        _chunk_kernel,
        grid=(b, s // CHK_STEP),
        in_specs=[pl.BlockSpec((1, CHK_STEP, C_WIDTH), lambda bi, i: (bi, i, 0)),
                  pl.BlockSpec((1, s, C_WIDTH), lambda bi, i: (bi, 0, 0)),
                  pl.BlockSpec((1, s, C_WIDTH), lambda bi, i: (bi, 0, 0)),
                  pl.BlockSpec(bias.shape, lambda bi, i: (0, 0, 0, 0),
                               pipeline_mode=pl.Buffered(1))],
        out_specs=pl.BlockSpec((1, CHK_STEP, C_WIDTH), lambda bi, i: (bi, i, 0)),
        out_shape=jax.ShapeDtypeStruct((b, s, C_WIDTH), F32),
        compiler_params=_params("arbitrary", "arbitrary"),
        name="chunk_attn",
    )(q, k, v, bias)


def _mem_kernel(m_ref, g_ref, w_ref, k_ref, v_ref):
    h = _rms(m_ref[0], g_ref[...], D_MODEL).astype(BF16)
    kv = jnp.dot(h, w_ref[...], preferred_element_type=F32)
    k_ref[0] = kv[:, :CROSS_WIDTH].astype(BF16)
    v_ref[0] = kv[:, CROSS_WIDTH:].astype(BF16)


def _mem_kv(mem, g, w, layer):
    b = mem.shape[0]
    spec = pl.BlockSpec((1, MEM_LEN, CROSS_WIDTH), lambda bi: (bi, 0, 0))
    shape = jax.ShapeDtypeStruct((b, MEM_LEN, CROSS_WIDTH), BF16)
    return pl.pallas_call(
        _mem_kernel,
        grid=(b,),
        in_specs=[pl.BlockSpec((1, MEM_LEN, D_MODEL), lambda bi: (bi, 0, 0)),
                  _const_spec((1, D_MODEL)), _layer_spec((D_MODEL, 2 * CROSS_WIDTH), layer)],
        out_specs=[spec, spec],
        out_shape=[shape, shape],
        compiler_params=_params("arbitrary"),
        name="mem_kv",
    )(mem, g, w)


def _mix_cross_kernel(x_ref, ya_ref, yb_ref, yc_ref, ga_ref, gb_ref, gc_ref, woa_ref, wob_ref,
                      woc_ref, gx_ref, wcq_ref, km_ref, vm_ref, wco_ref, o_ref):
    tm = x_ref.shape[1]
    th = tm // MIX_SPLIT
    lane = _lane_iota((th, LANES))
    heads = [slice(hd * CROSS_DIM, (hd + 1) * CROSS_DIM) for hd in range(CROSS_HEADS)]
    subs = [dict(rows=slice(r * th, (r + 1) * th)) for r in range(MIX_SPLIT)]

    def stage_mix(st):
        rows = st["rows"]
        ya = jnp.concatenate([ya_ref[0, hd, rows, :] for hd in range(MLA_HEADS)], axis=-1)
        yb = jnp.concatenate([jnp.where(lane < FOX_DIM, yb_ref[0, hd, rows, :], 0.0)
                              for hd in range(FOX_HEADS)], axis=-1)
        st["x1"] = (x_ref[0, rows, :]
                    + jnp.dot(_rms(ya, ga_ref[...], A_WIDTH).astype(BF16), woa_ref[...],
                              preferred_element_type=F32)
                    + jnp.dot(_rms(yb, gb_ref[...], B_WIDTH).astype(BF16), wob_ref[...],
                              preferred_element_type=F32)
                    + jnp.dot(_rms(yc_ref[0, rows, :], gc_ref[...], C_WIDTH).astype(BF16),
                              woc_ref[...], preferred_element_type=F32))

    def stage_query(st):
        h = _rms(st["x1"], gx_ref[...], D_MODEL).astype(BF16)
        st["q"] = (jnp.dot(h, wcq_ref[...], preferred_element_type=F32)
                   * (CROSS_DIM ** -0.5 * LOG2E)).astype(BF16)

    def stage_scores(st):
        q = st.pop("q")
        st["s"] = [lax.dot_general(q[:, sl], km_ref[0, :, sl], NT_DIMS,
                                   preferred_element_type=F32) for sl in heads]

    def stage_attend(st):
        outs = []
        for s, sl in zip(st.pop("s"), heads):
            m = jnp.max(s, axis=-1, keepdims=True)
            p = jnp.exp2(s - m)
            denom = jnp.sum(p, axis=-1, keepdims=True)
            outs.append(jnp.dot(p.astype(BF16), vm_ref[0, :, sl], preferred_element_type=F32)
                        / denom)
        st["o"] = jnp.concatenate(outs, axis=-1).astype(BF16)

    def stage_out(st):
        o_ref[0, st["rows"], :] = st.pop("x1") + jnp.dot(st.pop("o"), wco_ref[...],
                                                         preferred_element_type=F32)

    for stage in (stage_mix, stage_query, stage_scores, stage_attend, stage_out):
        for st in subs:
            stage(st)


def _mix_cross(x, ya, yb, yc, ga, gb, gc, wo, wob, gx, wcq, km, vm, wco, layer):
    b, s, _ = x.shape
    tm = TOK_TILE
    x_spec = pl.BlockSpec((1, tm, D_MODEL), lambda bi, i: (bi, i, 0))
    head_spec = pl.BlockSpec((1, 4, tm, LANES), lambda bi, i: (bi, 0, i, 0))
    mem_spec = pl.BlockSpec((1, MEM_LEN, CROSS_WIDTH), lambda bi, i: (bi, 0, 0))
    return pl.pallas_call(
        _mix_cross_kernel,
        grid=(b, s // tm),
        in_specs=[x_spec, head_spec, head_spec,
                  pl.BlockSpec((1, tm, C_WIDTH), lambda bi, i: (bi, i, 0)),
                  _const_spec((1, A_WIDTH)), _const_spec((1, 512)), _const_spec((1, C_WIDTH)),
                  _layer_spec((A_WIDTH, D_MODEL), layer), _const_spec((512, D_MODEL)),
                  _layer_spec((C_WIDTH, D_MODEL), layer, (A_WIDTH + B_WIDTH) // C_WIDTH),
                  _const_spec((1, D_MODEL)), _layer_spec((D_MODEL, CROSS_WIDTH), layer),
                  mem_spec, mem_spec, _layer_spec((CROSS_WIDTH, D_MODEL), layer)],
        out_specs=x_spec,
        out_shape=jax.ShapeDtypeStruct(x.shape, F32),
        compiler_params=_params("arbitrary", "arbitrary"),
        name="mix_cross",
    )(x, ya, yb, yc, ga, gb, gc, wo, wob, wo, gx, wcq, km, vm, wco)


def _ffn_kernel(x_ref, g_ref, wgu_ref, wd_ref, gf_ref, o_ref, acc_ref, *, final_norm):
    x = x_ref[0]
    h = _rms(x, g_ref[...], D_MODEL).astype(BF16)
    acc_ref[...] = x
    nblk = FFN_HIDDEN // FFN_BLOCK

    def gate_up(c):
        lo = c * FFN_BLOCK
        gate = jnp.dot(h, wgu_ref[:, lo:lo + FFN_BLOCK], preferred_element_type=F32)
        up = jnp.dot(h, wgu_ref[:, FFN_HIDDEN + lo:FFN_HIDDEN + lo + FFN_BLOCK],
                     preferred_element_type=F32)
        return gate, up

    nxt = gate_up(0)
    for c in range(nblk):
        gate, up = nxt
        if c + 1 < nblk:
            nxt = gate_up(c + 1)
        act = (gate * jax.nn.sigmoid(gate) * up).astype(BF16)
        acc_ref[...] += jnp.dot(act, wd_ref[c * FFN_BLOCK:(c + 1) * FFN_BLOCK, :],
                                preferred_element_type=F32)
    y = acc_ref[...]
    o_ref[0] = _rms(y, gf_ref[...], D_MODEL) if final_norm else y


def _ffn(x, g, wgu, wd, gf, layer, *, final_norm):
    b, s, _ = x.shape
    tm = TOK_TILE
    x_spec = pl.BlockSpec((1, tm, D_MODEL), lambda bi, i: (bi, i, 0))

    def resident(shape):
        return _layer_spec(shape, layer, pipeline_mode=pl.Buffered(1))

    return pl.pallas_call(
        functools.partial(_ffn_kernel, final_norm=final_norm),
        grid=(b, s // tm),
        in_specs=[x_spec, _const_spec((1, D_MODEL)),
                  resident((D_MODEL, 2 * FFN_HIDDEN)), resident((FFN_HIDDEN, D_MODEL)),
                  _const_spec((1, D_MODEL))],
        out_specs=x_spec,
        out_shape=jax.ShapeDtypeStruct(x.shape, F32),
        scratch_shapes=[pltpu.VMEM((tm, D_MODEL), F32)],
        compiler_params=_params("arbitrary", "arbitrary"),
        name="ffn",
    )(x, g, wgu, wd, gf)


def _pad_cols(a, width):
    return jnp.pad(a, ((0, 0), (0, width - a.shape[1])))


def _layer_weights(w_in, w_uq, w_ukv, w_o, out_norm, f_bias):
    sizes = (256, 128, 32, 256, 256, 256, 4, 256, 256, 256)
    parts, at = [], 0
    for n in sizes:
        parts.append(w_in[:, at:at + n])
        at += n
    c_q, c_kv, k_rope, fq, fk, fv, f_logit, cq, ck, cv = parts
    zeros = lambda n: jnp.zeros((D_MODEL, n), F32)
    misc = jnp.concatenate([f_logit, zeros(MLA_NOPE - FOX_HEADS), k_rope, zeros(32)], axis=1)
    w_in_r = jnp.concatenate([c_q, c_kv, misc, fq, fk, fv, cq, ck, cv], axis=1).astype(BF16)
    wuq = jnp.pad(w_uq.reshape(MLA_Q_RANK, MLA_HEADS, MLA_NOPE + MLA_ROPE),
                  ((0, 0), (0, 0), (0, 32))).reshape(MLA_Q_RANK, 512).astype(BF16)
    wukv = w_ukv.reshape(MLA_KV_RANK, MLA_HEADS, MLA_NOPE + MLA_V)
    wk = jnp.pad(wukv[:, :, :MLA_NOPE], ((0, 0), (0, 0), (0, 64))).reshape(MLA_KV_RANK, 512)
    wv = wukv[:, :, MLA_NOPE:].reshape(MLA_KV_RANK, 512)
    wukv_r = jnp.concatenate([wk, wv], axis=1).astype(BF16)
    wob = jnp.pad(w_o[A_WIDTH:A_WIDTH + B_WIDTH].reshape(FOX_HEADS, FOX_DIM, D_MODEL),
                  ((0, 0), (0, 64), (0, 0))).reshape(512, D_MODEL).astype(BF16)
    ga = out_norm[:A_WIDTH].reshape(1, A_WIDTH)
    gb = jnp.pad(out_norm[A_WIDTH:A_WIDTH + B_WIDTH].reshape(FOX_HEADS, FOX_DIM),
                 ((0, 0), (0, 64))).reshape(1, 512)
    gc = out_norm[A_WIDTH + B_WIDTH:].reshape(1, C_WIDTH)
    fb = _pad_cols(f_bias.reshape(1, FOX_HEADS), LANES)
    return w_in_r, wuq, wukv_r, wob, ga, gb, gc, fb


def _rope_tables(seq):
    pos = jnp.arange(seq, dtype=F32)
    inv = ROPE_THETA ** (-jnp.arange(0, MLA_ROPE, 2, dtype=F32) / MLA_ROPE)
    ang = pos[:, None] * inv[None, :]
    cos, sin = jnp.cos(ang), jnp.sin(ang)
    z = lambda n: jnp.zeros((seq, n), F32)
    half = MLA_ROPE // 2
    tcos = jnp.concatenate([z(MLA_NOPE), cos, cos, z(32)], axis=1)
    ts1 = jnp.concatenate([z(MLA_NOPE), -sin, z(half), z(32)], axis=1)
    ts2 = jnp.concatenate([z(MLA_NOPE), z(half), sin, z(32)], axis=1)
    return tcos, ts1, ts2


def _tile_bias(bias):
    nq = CHK_TILE // CHUNK
    rows = [jnp.pad(bias, ((0, 0), (0, 0), (c * CHUNK, CHK_WIN - BAND - c * CHUNK)),
                    constant_values=NEG) for c in range(nq)]
    tile = jnp.concatenate(rows, axis=1)
    lead = CHK_LEFT * CHUNK // CHK_TILE
    return jnp.stack([jnp.pad(tile[:, :, v * CHK_TILE:], ((0, 0), (0, 0), (0, v * CHK_TILE)),
                              constant_values=NEG) for v in range(lead + 1)])


def kernel(x, mem, norm_mix, w_in, q_norm, w_uq, kv_norm, w_ukv, f_bias, rel_bias, out_norm, w_o,
           norm_cross, norm_mem, w_cq, w_ckv, w_co, norm_ffn, w_gu, w_down, final_norm):
    b, s, _ = x.shape
    depth = w_in.shape[0]
    tcos, ts1, ts2 = _rope_tables(s)
    tri = jnp.tril(jnp.ones((TOK_TILE, TOK_TILE), BF16))
    row = lambda v: v.reshape(1, -1)
    wo_b, wcq_b, wckv_b, wco_b, wgu_b, wd_b = (w.astype(BF16)
                                                for w in (w_o, w_cq, w_ckv, w_co, w_gu, w_down))
    for l in range(depth):
        (w_in_r, wuq, wukv_r, wob, ga, gb, gc, fb) = _layer_weights(
            w_in[l], w_uq[l], w_ukv[l], w_o[l], out_norm[l], f_bias[l])
        mq, mk, mv, fq, fk, fv, cq, ck, cv, cum = _in_proj(
            x, row(norm_mix[l]), w_in_r, row(q_norm[l]), wuq, row(kv_norm[l]), wukv_r,
            tcos, ts1, ts2, fb, tri)
        ya = _sweep(mq, mk, mv, chunk_mask=True)
        cum_h = jnp.transpose(cum[:, :, :FOX_HEADS], (0, 2, 1))
        yb = _sweep(fq, fk, fv, cum, cum_h.reshape(b, FOX_HEADS, s // ATT_TILE, ATT_TILE),
                    chunk_mask=False)
        yc = _chunk_attn(cq, ck, cv, _tile_bias(_rel_bias(rel_bias[l])))
        km, vm = _mem_kv(mem, row(norm_mem[l]), wckv_b, l)
        x = _mix_cross(x, ya, yb, yc, ga, gb, gc, wo_b, wob, row(norm_cross[l]), wcq_b, km, vm,
                       wco_b, l)
        x = _ffn(x, row(norm_ffn[l]), wgu_b, wd_b, row(final_norm), l,
                 final_norm=(l == depth - 1))
    return x
```

```python
import functools

import jax
import jax.numpy as jnp
from jax import lax
from jax.experimental import pallas as pl
from jax.experimental.pallas import tpu as pltpu

D_MODEL = 1024
CHUNK = 64
MEM_LEN = 256
EPS = 1e-6
MLA_HEADS = 4
MLA_Q_RANK = 256
MLA_KV_RANK = 128
MLA_NOPE = 64
MLA_ROPE = 32
MLA_V = 128
ROPE_THETA = 10000.0
FOX_HEADS = 4
FOX_DIM = 64
CHK_HEADS = 4
CHK_DIM = 64
CHK_LEFT = 8
BAND = (CHK_LEFT + 1) * CHUNK
REL_MAX = 128
REL_SIZE = (CHUNK - 1) + REL_MAX + 1
A_WIDTH = MLA_HEADS * MLA_V
B_WIDTH = FOX_HEADS * FOX_DIM
C_WIDTH = CHK_HEADS * CHK_DIM
CROSS_HEADS = 4
CROSS_DIM = 128
CROSS_WIDTH = CROSS_HEADS * CROSS_DIM
FFN_HIDDEN = 2816

LANES = 128
LOG2E = 1.4426950408889634
NEG = -1e30
SKIP_LOG2 = 152.0
VMEM_LIMIT = 56 * 1024 * 1024

TOK_TILE = 512
ATT_TILE = 512
SWEEP_UNROLL = 4
MLA_GROUP = 2
CHK_TILE = 256
CHK_WIN = CHK_TILE + CHK_LEFT * CHUNK
CHK_STEP = 4 * CHK_TILE
FFN_BLOCK = 256
IN_COLS = 2048
IN_SPLIT = 2
MIX_SPLIT = 2

BF16 = jnp.bfloat16
F32 = jnp.float32
NT_DIMS = (((1,), (1,)), ((), ()))


def _params(*semantics):
    return pltpu.CompilerParams(dimension_semantics=semantics, vmem_limit_bytes=VMEM_LIMIT)


def _rms(xf, gain, width):
    return xf * lax.rsqrt(jnp.sum(xf * xf, axis=-1, keepdims=True) * (1.0 / width) + EPS) * gain


def _lane_iota(shape):
    return lax.broadcasted_iota(jnp.int32, shape, len(shape) - 1)


def _const_spec(shape):
    zeros = (0,) * len(shape)
    return pl.BlockSpec(shape, lambda *_: zeros)


def _layer_spec(shape, layer, block=0, **kwargs):
    index = (layer, block, 0)
    return pl.BlockSpec((None,) + tuple(shape), lambda *_: index, **kwargs)


def _in_proj_kernel(x_ref, g_ref, w_ref, qn_ref, wuq_ref, kvn_ref, wukv_ref, tc_ref, ts1_ref,
                    ts2_ref, fb_ref, tri_ref,
                    mq_ref, mk_ref, mv_ref, fq_ref, fk_ref, fv_ref, cq_ref, ck_ref, cv_ref,
                    cum_ref, carry_ref):
    i = pl.program_id(1)
    tm = x_ref.shape[1]
    th = tm // IN_SPLIT
    lane = _lane_iota((th, LANES))
    q_scale = (MLA_NOPE + MLA_ROPE) ** -0.5 * LOG2E
    f_scale = FOX_DIM ** -0.5 * LOG2E
    ones_col = jnp.where(lane == FOX_DIM, 1.0, 0.0)

    @pl.when(i == 0)
    def _():
        carry_ref[...] = jnp.zeros_like(carry_ref)

    subs = [dict(rows=slice(r * th, (r + 1) * th)) for r in range(IN_SPLIT)]

    def proj(st, lo, hi):
        return jnp.dot(st["h"], w_ref[:, lo:hi], preferred_element_type=F32)

    def rope(st, y, keep_low):
        rows = st["rows"]
        tcos = tc_ref[rows, :]
        base = tcos + jnp.where(lane < MLA_NOPE, 1.0, 0.0) if keep_low else tcos
        return (y * base + pltpu.roll(y, LANES - MLA_ROPE // 2, 1) * ts1_ref[rows, :]
                + pltpu.roll(y, MLA_ROPE // 2, 1) * ts2_ref[rows, :])

    def head_block(y, hd):
        blk = y[:, (hd // 2) * LANES:(hd // 2 + 1) * LANES]
        return pltpu.roll(blk, FOX_DIM, 1) if hd % 2 else blk

    def stage_norm(st):
        st["h"] = _rms(x_ref[0, st["rows"], :], g_ref[...], D_MODEL).astype(BF16)

    def stage_latent(st):
        st["c_q"] = proj(st, 0, 256)
        st["misc"] = proj(st, 384, 512)
        st["c_kv"] = proj(st, 256, 384)

    def stage_up(st):
        st["q_up"] = jnp.dot(_rms(st.pop("c_q"), qn_ref[...], MLA_Q_RANK).astype(BF16),
                             wuq_ref[...], preferred_element_type=F32)
        st["kv_up"] = jnp.dot(_rms(st.pop("c_kv"), kvn_ref[...], MLA_KV_RANK).astype(BF16),
                              wukv_ref[...], preferred_element_type=F32)

    def stage_fox_proj(st):
        st["fq"] = proj(st, 512, 768)
        st["fk"] = proj(st, 768, 1024)
        st["fv"] = proj(st, 1024, 1280)

    def stage_mla_store(st):
        rows = st["rows"]
        q_up, kv_up = st.pop("q_up"), st.pop("kv_up")
        k_pe = rope(st, st["misc"], False)
        for hd in range(MLA_HEADS):
            sl = slice(hd * LANES, (hd + 1) * LANES)
            mq_ref[0, hd, rows, :] = (rope(st, q_up[:, sl], True) * q_scale).astype(BF16)
            mk_ref[0, hd, rows, :] = (kv_up[:, sl] + k_pe).astype(BF16)
            mv_ref[0, hd, rows, :] = kv_up[:, A_WIDTH + hd * LANES:
                                           A_WIDTH + (hd + 1) * LANES].astype(BF16)

    def stage_chunk_proj(st):
        rows = st["rows"]
        cq_ref[0, rows, :] = (proj(st, 1280, 1536) * (CHK_DIM ** -0.5 * LOG2E)).astype(BF16)
        ck_ref[0, rows, :] = proj(st, 1536, 1792).astype(BF16)
        cv_ref[0, rows, :] = proj(st, 1792, 2048).astype(BF16)

    def stage_fox_store(st):
        rows = st["rows"]
        fq, fk, fv = st.pop("fq"), st.pop("fk"), st.pop("fv")
        for hd in range(FOX_HEADS):
            fq_ref[0, hd, rows, :] = jnp.where(lane < FOX_DIM, head_block(fq, hd) * f_scale,
                                               0.0).astype(BF16)
            fk_ref[0, hd, rows, :] = jnp.where(lane < FOX_DIM, head_block(fk, hd),
                                               0.0).astype(BF16)
            fv_ref[0, hd, rows, :] = jnp.where(lane < FOX_DIM, head_block(fv, hd),
                                               ones_col).astype(BF16)

    def stage_gate(st):
        z = st.pop("misc") + fb_ref[...]
        log_f = jnp.minimum(z, 0.0) - jnp.log1p(jnp.exp(-jnp.abs(z)))
        log_f = jnp.where(lane < FOX_HEADS, log_f, 0.0)
        p_hi = log_f.astype(BF16)
        rem = log_f - p_hi.astype(F32)
        p_mid = rem.astype(BF16)
        p_lo = (rem - p_mid.astype(F32)).astype(BF16)
        parts = jnp.dot(tri_ref[:th, :th], jnp.concatenate([p_hi, p_mid, p_lo], axis=-1),
                        preferred_element_type=F32)
        st["cum"] = parts[:, :LANES] + parts[:, LANES:2 * LANES] + parts[:, 2 * LANES:]

    for stage in (stage_norm, stage_latent, stage_up, stage_fox_proj, stage_mla_store,
                  stage_chunk_proj, stage_fox_store, stage_gate):
        for st in subs:
            stage(st)

    offset = carry_ref[...]
    for st in subs:
        cum = st["cum"] + offset
        offset = cum[th - 1:th, :]
        cum_ref[0, st["rows"], :] = cum * LOG2E
    carry_ref[...] = offset


def _in_proj(x, g, w, qn, wuq, kvn, wukv, tcos, ts1, ts2, fb, tri):
    b, s, _ = x.shape
    tm = TOK_TILE
    head_spec = pl.BlockSpec((1, 4, tm, LANES), lambda bi, i: (bi, 0, i, 0))
    dense_spec = pl.BlockSpec((1, tm, 256), lambda bi, i: (bi, i, 0))
    tab_spec = pl.BlockSpec((tm, LANES), lambda bi, i: (i, 0))
    head_shape = jax.ShapeDtypeStruct((b, 4, s, LANES), BF16)
    dense_shape = jax.ShapeDtypeStruct((b, s, 256), BF16)
    return pl.pallas_call(
        _in_proj_kernel,
        grid=(b, s // tm),
        in_specs=[pl.BlockSpec((1, tm, D_MODEL), lambda bi, i: (bi, i, 0)),
                  _const_spec((1, D_MODEL)), _const_spec((D_MODEL, IN_COLS)),
                  _const_spec((1, MLA_Q_RANK)), _const_spec((MLA_Q_RANK, 512)),
                  _const_spec((1, MLA_KV_RANK)), _const_spec((MLA_KV_RANK, 1024)),
                  tab_spec, tab_spec, tab_spec, _const_spec((1, LANES)), _const_spec((tm, tm))],
        out_specs=[head_spec] * 6 + [dense_spec] * 3
                  + [pl.BlockSpec((1, tm, LANES), lambda bi, i: (bi, i, 0))],
        out_shape=[head_shape] * 6 + [dense_shape] * 3
                  + [jax.ShapeDtypeStruct((b, s, LANES), F32)],
        scratch_shapes=[pltpu.VMEM((1, LANES), F32)],
        compiler_params=_params("arbitrary", "arbitrary"),
        name="in_proj",
    )(x, g, w, qn, wuq, kvn, wukv, tcos, ts1, ts2, fb, tri)


def _sweep_kernel(*refs, decay, chunk_mask, group):
    if decay:
        (q_ref, k_ref, v_ref, cq_ref, ck_ref, o_ref,
         m_sc, l_sc, acc_sc, s_sc, p_sc, alpha_sc, kmax_sc, ckmin_sc) = refs
    else:
        q_ref, k_ref, v_ref, o_ref, m_sc, l_sc, acc_sc, s_sc, p_sc, alpha_sc = refs
    i = pl.program_id(2)
    t = ATT_TILE
    heads = range(group)
    m_sc[...] = jnp.full_like(m_sc, NEG)
    l_sc[...] = jnp.zeros_like(l_sc)
    acc_sc[...] = jnp.zeros_like(acc_sc)
    if decay:
        nblk = ck_ref.shape[2]

        @pl.when(i == 0)
        def _():
            def key_norm(j, acc):
                kf = k_ref[0, 0, pl.ds(pl.multiple_of(j * t, t), t), :].astype(F32)
                return jnp.maximum(acc, jnp.max(jnp.sum(kf * kf, axis=-1, keepdims=True),
                                                axis=0, keepdims=True))
            kmax_sc[...] = jnp.sqrt(lax.fori_loop(0, nblk, key_norm, jnp.zeros((1, 1), F32)))
            ckmin_sc[...] = jnp.min(ck_ref[0, 0], axis=-1, keepdims=True)

        head_lane = _lane_iota((t, LANES)) == pl.program_id(1)
        cq = jnp.sum(jnp.where(head_lane, cq_ref[0], 0.0), axis=-1, keepdims=True)
        cq_rep = jnp.broadcast_to(cq, (t, LANES))

    def first_needed_block():
        if not decay:
            return 0
        qf = q_ref[0, 0].astype(F32)
        kd = k_ref[0, 0, pl.ds(pl.multiple_of(i * t, t), t), :].astype(F32)
        gap = (jnp.sqrt(jnp.sum(qf * qf, axis=-1, keepdims=True)) * kmax_sc[...]
               - jnp.sum(qf * kd, axis=-1, keepdims=True) + cq)
        gap_max = jnp.max(gap, axis=0, keepdims=True)
        needed = gap_max - ckmin_sc[...] >= -SKIP_LOG2
        blk = lax.broadcasted_iota(jnp.int32, (nblk, 1), 0)
        return jnp.min(jnp.where(needed, blk, i))

    def scores(j, slot):
        off = pl.multiple_of(j * t, t)
        for g in heads:
            k = k_ref[0, g, pl.ds(off, t), :]
            s_sc[g, slot] = lax.dot_general(q_ref[0, g], k, NT_DIMS, preferred_element_type=F32)

    def softmax(j, slot, masked, void=None):
        for g in heads:
            s = s_sc[g, slot]
            if decay:
                s = s - ck_ref[0, 0, pl.ds(j, 1), :]
            if void is not None:
                s = s + jnp.where(void, NEG, 0.0)
            if masked:
                qi = lax.broadcasted_iota(jnp.int32, (t, t), 0)
                ki = lax.broadcasted_iota(jnp.int32, (t, t), 1)
                if chunk_mask:
                    qi, ki = qi // CHUNK, ki // CHUNK
                s = jnp.where(ki <= qi, s, NEG)
            m_old = m_sc[g]
            m_cur = jnp.max(s, axis=-1, keepdims=True)
            if decay:
                m_cur = m_cur + cq_rep
            m_new = jnp.maximum(m_old, m_cur)
            alpha = jnp.exp2(m_old - m_new)
            shift = m_new - cq_rep if decay else m_new
            ps = [jnp.exp2(s[:, c * LANES:(c + 1) * LANES] - shift) for c in range(t // LANES)]
            if not decay:
                l_sc[g] = alpha * l_sc[g] + functools.reduce(lambda a, b: a + b, ps)
            p_sc[g, slot] = jnp.concatenate(ps, axis=-1).astype(BF16)
            alpha_sc[g, slot] = alpha
            m_sc[g] = m_new

    def weigh(j, slot):
        off = pl.multiple_of(j * t, t)
        for g in heads:
            v = v_ref[0, g, pl.ds(off, t), :]
            acc_sc[g] = alpha_sc[g, slot] * acc_sc[g] + jnp.dot(p_sc[g, slot], v,
                                                                preferred_element_type=F32)

    scores(i, 0)
    if decay:
        prev = jnp.maximum(i - 1, 0)
        scores(prev, 1)
        softmax(i, 0, True)
        first = first_needed_block()
        n_full = 1 + jnp.maximum(i - 1 - first, 0)
        block = lambda w: jnp.where(w == 1, prev, first + w - 2)
        void = lambda w: jnp.logical_and(i == 0, w == 1)
    else:
        scores(0, 1)
        softmax(i, 0, True)
        n_full = i
        block = lambda w: w - 1
        void = lambda w: None
    unroll = SWEEP_UNROLL
    trips = jnp.maximum(n_full - 1, 0) // unroll

    def body(tt, carry):
        w = 1 + unroll * tt
        weigh(jnp.where(tt == 0, i, block(w - 1)), 0)
        for u in range(unroll):
            scores(block(w + u + 1), u % 2)
            softmax(block(w + u), (u + 1) % 2, False, void(w + u) if u == 0 else None)
            if u + 1 < unroll:
                weigh(block(w + u), (u + 1) % 2)
        return carry

    lax.fori_loop(0, trips, body, 0)
    left = n_full - unroll * trips
    w_left = 1 + unroll * trips
    j_pending = jnp.where(trips == 0, i, block(w_left - 1))

    def tail(count):
        weigh(j_pending, 0)
        for u in range(count):
            if u + 1 < count:
                scores(block(w_left + u + 1), u % 2)
            softmax(block(w_left + u), (u + 1) % 2, False, void(w_left + u) if u == 0 else None)
            weigh(block(w_left + u), (u + 1) % 2)

    for count in range(unroll + 1):
        pl.when(left == count)(functools.partial(tail, count))

    for g in heads:
        acc = acc_sc[g]
        if decay:
            denom = acc[:, FOX_DIM:FOX_DIM + 1]
        else:
            denom = jnp.sum(l_sc[g], axis=-1, keepdims=True)
        o_ref[0, g] = acc / denom


def _sweep(q, k, v, cum=None, ck=None, *, chunk_mask):
    b, nh, s, _ = q.shape
    t = ATT_TILE
    decay = cum is not None
    group = 1 if decay else MLA_GROUP
    q_spec = pl.BlockSpec((1, group, t, LANES), lambda bi, hi, i: (bi, hi, i, 0))
    kv_spec = pl.BlockSpec((1, group, s, LANES), lambda bi, hi, i: (bi, hi, 0, 0))
    in_specs = [q_spec, kv_spec, kv_spec]
    args = [q, k, v]
    scratch = [pltpu.VMEM((group, t, LANES), F32), pltpu.VMEM((group, t, LANES), F32),
               pltpu.VMEM((group, t, LANES), F32), pltpu.VMEM((group, 2, t, t), F32),
               pltpu.VMEM((group, 2, t, t), BF16), pltpu.VMEM((group, 2, t, LANES), F32)]
    if decay:
        in_specs += [pl.BlockSpec((1, t, LANES), lambda bi, hi, i: (bi, i, 0)),
                     pl.BlockSpec((1, 1, s // t, t), lambda bi, hi, i: (bi, hi, 0, 0))]
        args += [cum, ck]
        scratch += [pltpu.VMEM((1, 1), F32), pltpu.VMEM((s // t, 1), F32)]
    return pl.pallas_call(
        functools.partial(_sweep_kernel, decay=decay, chunk_mask=chunk_mask, group=group),
        grid=(b, nh // group, s // t),
        in_specs=in_specs,
        out_specs=q_spec,
        out_shape=jax.ShapeDtypeStruct((b, nh, s, LANES), F32),
        scratch_shapes=scratch,
        compiler_params=_params("arbitrary", "arbitrary", "arbitrary"),
        name="fox_sweep" if decay else "mla_sweep",
    )(*args)


def _rel_bias_kernel(tab_ref, o_ref):
    qi = lax.broadcasted_iota(jnp.int32, (CHUNK, BAND), 0)
    ki = lax.broadcasted_iota(jnp.int32, (CHUNK, BAND), 1)
    idx = jnp.clip(qi + CHK_LEFT * CHUNK - ki, -(CHUNK - 1), REL_MAX) + (CHUNK - 1)
    for hd in range(CHK_HEADS):
        def body(r, acc):
            return jnp.where(idx == r, tab_ref[hd, r], acc)
        o_ref[hd] = lax.fori_loop(0, REL_SIZE, body, jnp.zeros((CHUNK, BAND), F32)) * LOG2E


def _rel_bias(table):
    return pl.pallas_call(
        _rel_bias_kernel,
        in_specs=[pl.BlockSpec(memory_space=pltpu.SMEM)],
        out_specs=pl.BlockSpec(memory_space=pltpu.VMEM),
        out_shape=jax.ShapeDtypeStruct((CHK_HEADS, CHUNK, BAND), F32),
        name="rel_bias",
    )(table)


def _chunk_kernel(q_ref, k_ref, v_ref, bias_ref, o_ref):
    lane = _lane_iota((CHK_TILE, C_WIDTH)) // CHK_DIM
    lead = CHK_LEFT * CHUNK // CHK_TILE
    tiles = CHK_STEP // CHK_TILE
    units = []
    for u in range(tiles):
        tile = pl.program_id(1) * tiles + u
        start = pl.multiple_of(jnp.maximum(tile - lead, 0) * CHK_TILE, CHK_TILE)
        shift = jnp.maximum(lead - tile, 0)
        units += [(u, hd, start, shift) for hd in range(CHK_HEADS)]

    def scores(unit):
        u, hd, start, _ = unit
        q = q_ref[0, u * CHK_TILE:(u + 1) * CHK_TILE, :]
        kw = k_ref[0, pl.ds(start, CHK_WIN), :]
        return lax.dot_general(jnp.where(lane == hd, q, jnp.zeros((), q.dtype)), kw, NT_DIMS,
                               preferred_element_type=F32)

    def attend(unit, s):
        _, hd, start, shift = unit
        s = s + bias_ref[shift, hd]
        m = jnp.max(s, axis=-1, keepdims=True)
        p = jnp.exp2(s - m)
        denom = jnp.sum(p, axis=-1, keepdims=True)
        vw = v_ref[0, pl.ds(start, CHK_WIN), :]
        full = jnp.dot(p.astype(BF16), vw, preferred_element_type=F32)
        return jnp.where(lane == hd, full / denom, 0.0)

    outs = [None] * tiles
    s_next = scores(units[0])
    for n, unit in enumerate(units):
        u, hd = unit[:2]
        s_cur = s_next
        if n + 1 < len(units):
            s_next = scores(units[n + 1])
        o = attend(unit, s_cur)
        outs[u] = o if outs[u] is None else outs[u] + o
        if hd == CHK_HEADS - 1:
            o_ref[0, u * CHK_TILE:(u + 1) * CHK_TILE, :] = outs[u]


def _chunk_attn(q, k, v, bias):
    b, s, _ = q.shape
    return pl.pallas_call(
        _chunk_kernel,
        grid=(b, s // CHK_STEP),
        in_specs=[pl.BlockSpec((1, CHK_STEP, C_WIDTH), lambda bi, i: (bi, i, 0)),
                  pl.BlockSpec((1, s, C_WIDTH), lambda bi, i: (bi, 0, 0)),
                  pl.BlockSpec((1, s, C_WIDTH), lambda bi, i: (bi, 0, 0)),
                  pl.BlockSpec(bias.shape, lambda bi, i: (0, 0, 0, 0),
                               pipeline_mode=pl.Buffered(1))],
        out_specs=pl.BlockSpec((1, CHK_STEP, C_WIDTH), lambda bi, i: (bi, i, 0)),
        out_shape=jax.ShapeDtypeStruct((b, s, C_WIDTH), F32),
        compiler_params=_params("arbitrary", "arbitrary"),
        name="chunk_attn",
    )(q, k, v, bias)


def _mem_kernel(m_ref, g_ref, w_ref, k_ref, v_ref):
    h = _rms(m_ref[0], g_ref[...], D_MODEL).astype(BF16)
    kv = jnp.dot(h, w_ref[...], preferred_element_type=F32)
    k_ref[0] = kv[:, :CROSS_WIDTH].astype(BF16)
    v_ref[0] = kv[:, CROSS_WIDTH:].astype(BF16)


def _mem_kv(mem, g, w, layer):
    b = mem.shape[0]
    spec = pl.BlockSpec((1, MEM_LEN, CROSS_WIDTH), lambda bi: (bi, 0, 0))
    shape = jax.ShapeDtypeStruct((b, MEM_LEN, CROSS_WIDTH), BF16)
    return pl.pallas_call(
        _mem_kernel,
        grid=(b,),
        in_specs=[pl.BlockSpec((1, MEM_LEN, D_MODEL), lambda bi: (bi, 0, 0)),
                  _const_spec((1, D_MODEL)), _layer_spec((D_MODEL, 2 * CROSS_WIDTH), layer)],
        out_specs=[spec, spec],
        out_shape=[shape, shape],
        compiler_params=_params("arbitrary"),
        name="mem_kv",
    )(mem, g, w)


def _mix_cross_kernel(x_ref, ya_ref, yb_ref, yc_ref, ga_ref, gb_ref, gc_ref, woa_ref, wob_ref,
                      woc_ref, gx_ref, wcq_ref, km_ref, vm_ref, wco_ref, o_ref):
    tm = x_ref.shape[1]
    th = tm // MIX_SPLIT
    lane = _lane_iota((th, LANES))
    heads = [slice(hd * CROSS_DIM, (hd + 1) * CROSS_DIM) for hd in range(CROSS_HEADS)]
    subs = [dict(rows=slice(r * th, (r + 1) * th)) for r in range(MIX_SPLIT)]

    def stage_mix(st):
        rows = st["rows"]
        ya = jnp.concatenate([ya_ref[0, hd, rows, :] for hd in range(MLA_HEADS)], axis=-1)
        yb = jnp.concatenate([jnp.where(lane < FOX_DIM, yb_ref[0, hd, rows, :], 0.0)
                              for hd in range(FOX_HEADS)], axis=-1)
        st["x1"] = (x_ref[0, rows, :]
                    + jnp.dot(_rms(ya, ga_ref[...], A_WIDTH).astype(BF16), woa_ref[...],
                              preferred_element_type=F32)
                    + jnp.dot(_rms(yb, gb_ref[...], B_WIDTH).astype(BF16), wob_ref[...],
                              preferred_element_type=F32)
                    + jnp.dot(_rms(yc_ref[0, rows, :], gc_ref[...], C_WIDTH).astype(BF16),
                              woc_ref[...], preferred_element_type=F32))

    def stage_query(st):
        h = _rms(st["x1"], gx_ref[...], D_MODEL).astype(BF16)
        st["q"] = (jnp.dot(h, wcq_ref[...], preferred_element_type=F32)
                   * (CROSS_DIM ** -0.5 * LOG2E)).astype(BF16)

    def stage_scores(st):
        q = st.pop("q")
        st["s"] = [lax.dot_general(q[:, sl], km_ref[0, :, sl], NT_DIMS,
                                   preferred_element_type=F32) for sl in heads]

    def stage_attend(st):
        outs = []
        for s, sl in zip(st.pop("s"), heads):
            m = jnp.max(s, axis=-1, keepdims=True)
            p = jnp.exp2(s - m)
            denom = jnp.sum(p, axis=-1, keepdims=True)
            outs.append(jnp.dot(p.astype(BF16), vm_ref[0, :, sl], preferred_element_type=F32)
                        / denom)
        st["o"] = jnp.concatenate(outs, axis=-1).astype(BF16)

    def stage_out(st):
        o_ref[0, st["rows"], :] = st.pop("x1") + jnp.dot(st.pop("o"), wco_ref[...],
                                                         preferred_element_type=F32)

    for stage in (stage_mix, stage_query, stage_scores, stage_attend, stage_out):
        for st in subs:
            stage(st)


def _mix_cross(x, ya, yb, yc, ga, gb, gc, wo, wob, gx, wcq, km, vm, wco, layer):
    b, s, _ = x.shape
    tm = TOK_TILE
    x_spec = pl.BlockSpec((1, tm, D_MODEL), lambda bi, i: (bi, i, 0))
    head_spec = pl.BlockSpec((1, 4, tm, LANES), lambda bi, i: (bi, 0, i, 0))
    mem_spec = pl.BlockSpec((1, MEM_LEN, CROSS_WIDTH), lambda bi, i: (bi, 0, 0))
    return pl.pallas_call(
        _mix_cross_kernel,
        grid=(b, s // tm),
        in_specs=[x_spec, head_spec, head_spec,
                  pl.BlockSpec((1, tm, C_WIDTH), lambda bi, i: (bi, i, 0)),
                  _const_spec((1, A_WIDTH)), _const_spec((1, 512)), _const_spec((1, C_WIDTH)),
                  _layer_spec((A_WIDTH, D_MODEL), layer), _const_spec((512, D_MODEL)),
                  _layer_spec((C_WIDTH, D_MODEL), layer, (A_WIDTH + B_WIDTH) // C_WIDTH),
                  _const_spec((1, D_MODEL)), _layer_spec((D_MODEL, CROSS_WIDTH), layer),
                  mem_spec, mem_spec, _layer_spec((CROSS_WIDTH, D_MODEL), layer)],
        out_specs=x_spec,
        out_shape=jax.ShapeDtypeStruct(x.shape, F32),
        compiler_params=_params("arbitrary", "arbitrary"),
        name="mix_cross",
    )(x, ya, yb, yc, ga, gb, gc, wo, wob, wo, gx, wcq, km, vm, wco)


def _ffn_kernel(x_ref, g_ref, wgu_ref, wd_ref, gf_ref, o_ref, acc_ref, *, final_norm):
    x = x_ref[0]
    h = _rms(x, g_ref[...], D_MODEL).astype(BF16)
    acc_ref[...] = x
    nblk = FFN_HIDDEN // FFN_BLOCK

    def gate_up(c):
        lo = c * FFN_BLOCK
        gate = jnp.dot(h, wgu_ref[:, lo:lo + FFN_BLOCK], preferred_element_type=F32)
        up = jnp.dot(h, wgu_ref[:, FFN_HIDDEN + lo:FFN_HIDDEN + lo + FFN_BLOCK],
                     preferred_element_type=F32)
        return gate, up

    nxt = gate_up(0)
    for c in range(nblk):
        gate, up = nxt
        if c + 1 < nblk:
            nxt = gate_up(c + 1)
        act = (gate * jax.nn.sigmoid(gate) * up).astype(BF16)
        acc_ref[...] += jnp.dot(act, wd_ref[c * FFN_BLOCK:(c + 1) * FFN_BLOCK, :],
                                preferred_element_type=F32)
    y = acc_ref[...]
    o_ref[0] = _rms(y, gf_ref[...], D_MODEL) if final_norm else y


def _ffn(x, g, wgu, wd, gf, layer, *, final_norm):
    b, s, _ = x.shape
    tm = TOK_TILE
    x_spec = pl.BlockSpec((1, tm, D_MODEL), lambda bi, i: (bi, i, 0))

    def resident(shape):
        return _layer_spec(shape, layer, pipeline_mode=pl.Buffered(1))

    return pl.pallas_call(
        functools.partial(_ffn_kernel, final_norm=final_norm),
        grid=(b, s // tm),
        in_specs=[x_spec, _const_spec((1, D_MODEL)),
                  resident((D_MODEL, 2 * FFN_HIDDEN)), resident((FFN_HIDDEN, D_MODEL)),
                  _const_spec((1, D_MODEL))],
        out_specs=x_spec,
        out_shape=jax.ShapeDtypeStruct(x.shape, F32),
        scratch_shapes=[pltpu.VMEM((tm, D_MODEL), F32)],
        compiler_params=_params("arbitrary", "arbitrary"),
        name="ffn",
    )(x, g, wgu, wd, gf)


def _pad_cols(a, width):
    return jnp.pad(a, ((0, 0), (0, width - a.shape[1])))


def _layer_weights(w_in, w_uq, w_ukv, w_o, out_norm, f_bias):
    sizes = (256, 128, 32, 256, 256, 256, 4, 256, 256, 256)
    parts, at = [], 0
    for n in sizes:
        parts.append(w_in[:, at:at + n])
        at += n
    c_q, c_kv, k_rope, fq, fk, fv, f_logit, cq, ck, cv = parts
    zeros = lambda n: jnp.zeros((D_MODEL, n), F32)
    misc = jnp.concatenate([f_logit, zeros(MLA_NOPE - FOX_HEADS), k_rope, zeros(32)], axis=1)
    w_in_r = jnp.concatenate([c_q, c_kv, misc, fq, fk, fv, cq, ck, cv], axis=1).astype(BF16)
    wuq = jnp.pad(w_uq.reshape(MLA_Q_RANK, MLA_HEADS, MLA_NOPE + MLA_ROPE),
                  ((0, 0), (0, 0), (0, 32))).reshape(MLA_Q_RANK, 512).astype(BF16)
    wukv = w_ukv.reshape(MLA_KV_RANK, MLA_HEADS, MLA_NOPE + MLA_V)
    wk = jnp.pad(wukv[:, :, :MLA_NOPE], ((0, 0), (0, 0), (0, 64))).reshape(MLA_KV_RANK, 512)
    wv = wukv[:, :, MLA_NOPE:].reshape(MLA_KV_RANK, 512)
    wukv_r = jnp.concatenate([wk, wv], axis=1).astype(BF16)
    wob = jnp.pad(w_o[A_WIDTH:A_WIDTH + B_WIDTH].reshape(FOX_HEADS, FOX_DIM, D_MODEL),
                  ((0, 0), (0, 64), (0, 0))).reshape(512, D_MODEL).astype(BF16)
    ga = out_norm[:A_WIDTH].reshape(1, A_WIDTH)
    gb = jnp.pad(out_norm[A_WIDTH:A_WIDTH + B_WIDTH].reshape(FOX_HEADS, FOX_DIM),
                 ((0, 0), (0, 64))).reshape(1, 512)
    gc = out_norm[A_WIDTH + B_WIDTH:].reshape(1, C_WIDTH)
    fb = _pad_cols(f_bias.reshape(1, FOX_HEADS), LANES)
    return w_in_r, wuq, wukv_r, wob, ga, gb, gc, fb


def _rope_tables(seq):
    pos = jnp.arange(seq, dtype=F32)
    inv = ROPE_THETA ** (-jnp.arange(0, MLA_ROPE, 2, dtype=F32) / MLA_ROPE)
    ang = pos[:, None] * inv[None, :]
    cos, sin = jnp.cos(ang), jnp.sin(ang)
    z = lambda n: jnp.zeros((seq, n), F32)
    half = MLA_ROPE // 2
    tcos = jnp.concatenate([z(MLA_NOPE), cos, cos, z(32)], axis=1)
    ts1 = jnp.concatenate([z(MLA_NOPE), -sin, z(half), z(32)], axis=1)
    ts2 = jnp.concatenate([z(MLA_NOPE), z(half), sin, z(32)], axis=1)
    return tcos, ts1, ts2


def _tile_bias(bias):
    nq = CHK_TILE // CHUNK
    rows = [jnp.pad(bias, ((0, 0), (0, 0), (c * CHUNK, CHK_WIN - BAND - c * CHUNK)),
                    constant_values=NEG) for c in range(nq)]
    tile = jnp.concatenate(rows, axis=1)
    lead = CHK_LEFT * CHUNK // CHK_TILE
    return jnp.stack([jnp.pad(tile[:, :, v * CHK_TILE:], ((0, 0), (0, 0), (0, v * CHK_TILE)),
                              constant_values=NEG) for v in range(lead + 1)])


def kernel(x, mem, norm_mix, w_in, q_norm, w_uq, kv_norm, w_ukv, f_bias, rel_bias, out_norm, w_o,
           norm_cross, norm_mem, w_cq, w_ckv, w_co, norm_ffn, w_gu, w_down, final_norm):
    b, s, _ = x.shape
    depth = w_in.shape[0]
    tcos, ts1, ts2 = _rope_tables(s)
    tri = jnp.tril(jnp.ones((TOK_TILE, TOK_TILE), BF16))
    row = lambda v: v.reshape(1, -1)
    wo_b, wcq_b, wckv_b, wco_b, wgu_b, wd_b = (w.astype(BF16)
                                                for w in (w_o, w_cq, w_ckv, w_co, w_gu, w_down))
    for l in range(depth):
        (w_in_r, wuq, wukv_r, wob, ga, gb, gc, fb) = _layer_weights(
            w_in[l], w_uq[l], w_ukv[l], w_o[l], out_norm[l], f_bias[l])
        mq, mk, mv, fq, fk, fv, cq, ck, cv, cum = _in_proj(
            x, row(norm_mix[l]), w_in_r, row(q_norm[l]), wuq, row(kv_norm[l]), wukv_r,
            tcos, ts1, ts2, fb, tri)
        ya = _sweep(mq, mk, mv, chunk_mask=True)
        cum_h = jnp.transpose(cum[:, :, :FOX_HEADS], (0, 2, 1))
        yb = _sweep(fq, fk, fv, cum, cum_h.reshape(b, FOX_HEADS, s // ATT_TILE, ATT_TILE),
                    chunk_mask=False)
        yc = _chunk_attn(cq, ck, cv, _tile_bias(_rel_bias(rel_bias[l])))
        km, vm = _mem_kv(mem, row(norm_mem[l]), wckv_b, l)
        x = _mix_cross(x, ya, yb, yc, ga, gb, gc, wo_b, wob, row(norm_cross[l]), wcq_b, km, vm,
                       wco_b, l)
        x = _ffn(x, row(norm_ffn[l]), wgu_b, wd_b, row(final_norm), l,
                 final_norm=(l == depth - 1))
    return x
```

```python
import functools

import jax
import jax.numpy as jnp
from jax import lax
from jax.experimental import pallas as pl
from jax.experimental.pallas import tpu as pltpu

D_MODEL = 1024
CHUNK = 64
MEM_LEN = 256
EPS = 1e-6
MLA_HEADS = 4
MLA_Q_RANK = 256
MLA_KV_RANK = 128
MLA_NOPE = 64
MLA_ROPE = 32
MLA_V = 128
ROPE_THETA = 10000.0
FOX_HEADS = 4
FOX_DIM = 64
CHK_HEADS = 4
CHK_DIM = 64
CHK_LEFT = 8
BAND = (CHK_LEFT + 1) * CHUNK
REL_MAX = 128
REL_SIZE = (CHUNK - 1) + REL_MAX + 1
A_WIDTH = MLA_HEADS * MLA_V
B_WIDTH = FOX_HEADS * FOX_DIM
C_WIDTH = CHK_HEADS * CHK_DIM
CROSS_HEADS = 4
CROSS_DIM = 128
CROSS_WIDTH = CROSS_HEADS * CROSS_DIM
FFN_HIDDEN = 2816

LANES = 128
LOG2E = 1.4426950408889634
NEG = -1e30
SKIP_LOG2 = 152.0
VMEM_LIMIT = 56 * 1024 * 1024

TOK_TILE = 1024
ATT_TILE = 512
SWEEP_UNROLL = 4
MLA_GROUP = 2
CHK_TILE = 256
CHK_WIN = CHK_TILE + CHK_LEFT * CHUNK
CHK_STEP = 8 * CHK_TILE
FFN_BLOCK = 256
IN_COLS = 2048
IN_SPLIT = 4
MIX_SPLIT = 4

BF16 = jnp.bfloat16
F32 = jnp.float32
NT_DIMS = (((1,), (1,)), ((), ()))


def _params(*semantics):
    return pltpu.CompilerParams(dimension_semantics=semantics, vmem_limit_bytes=VMEM_LIMIT)


def _rms(xf, gain, width):
    return xf * lax.rsqrt(jnp.sum(xf * xf, axis=-1, keepdims=True) * (1.0 / width) + EPS) * gain


def _lane_iota(shape):
    return lax.broadcasted_iota(jnp.int32, shape, len(shape) - 1)


def _const_spec(shape):
    zeros = (0,) * len(shape)
    return pl.BlockSpec(shape, lambda *_: zeros)


def _layer_spec(shape, layer, block=0, **kwargs):
    index = (layer, block, 0)
    return pl.BlockSpec((None,) + tuple(shape), lambda *_: index, **kwargs)


def _in_proj_kernel(x_ref, g_ref, w_ref, qn_ref, wuq_ref, kvn_ref, wukv_ref, tc_ref, ts1_ref,
                    ts2_ref, fb_ref, tri_ref,
                    mq_ref, mk_ref, mv_ref, fq_ref, fk_ref, fv_ref, cq_ref, ck_ref, cv_ref,
                    cum_ref, carry_ref):
    i = pl.program_id(1)
    tm = x_ref.shape[1]
    th = tm // IN_SPLIT
    lane = _lane_iota((th, LANES))
    q_scale = (MLA_NOPE + MLA_ROPE) ** -0.5 * LOG2E
    f_scale = FOX_DIM ** -0.5 * LOG2E
    ones_col = jnp.where(lane == FOX_DIM, 1.0, 0.0)

    @pl.when(i == 0)
    def _():
        carry_ref[...] = jnp.zeros_like(carry_ref)

    subs = [dict(rows=slice(r * th, (r + 1) * th)) for r in range(IN_SPLIT)]

    def proj(st, lo, hi):
        return jnp.dot(st["h"], w_ref[:, lo:hi], preferred_element_type=F32)

    def rope(st, y, keep_low):
        rows = st["rows"]
        tcos = tc_ref[rows, :]
        base = tcos + jnp.where(lane < MLA_NOPE, 1.0, 0.0) if keep_low else tcos
        return (y * base + pltpu.roll(y, LANES - MLA_ROPE // 2, 1) * ts1_ref[rows, :]
                + pltpu.roll(y, MLA_ROPE // 2, 1) * ts2_ref[rows, :])

    def head_block(y, hd):
        blk = y[:, (hd // 2) * LANES:(hd // 2 + 1) * LANES]
        return pltpu.roll(blk, FOX_DIM, 1) if hd % 2 else blk

    def stage_norm(st):
        st["h"] = _rms(x_ref[0, st["rows"], :], g_ref[...], D_MODEL).astype(BF16)

    def stage_latent(st):
        st["c_q"] = proj(st, 0, 256)
        st["misc"] = proj(st, 384, 512)
        st["c_kv"] = proj(st, 256, 384)

    def stage_up(st):
        st["q_up"] = jnp.dot(_rms(st.pop("c_q"), qn_ref[...], MLA_Q_RANK).astype(BF16),
                             wuq_ref[...], preferred_element_type=F32)
        st["kv_up"] = jnp.dot(_rms(st.pop("c_kv"), kvn_ref[...], MLA_KV_RANK).astype(BF16),
                              wukv_ref[...], preferred_element_type=F32)

    def stage_fox_proj(st):
        st["fq"] = proj(st, 512, 768)
        st["fk"] = proj(st, 768, 1024)
        st["fv"] = proj(st, 1024, 1280)

    def stage_mla_store(st):
        rows = st["rows"]
        q_up, kv_up = st.pop("q_up"), st.pop("kv_up")
        k_pe = rope(st, st["misc"], False)
        for hd in range(MLA_HEADS):
            sl = slice(hd * LANES, (hd + 1) * LANES)
            mq_ref[0, hd, rows, :] = (rope(st, q_up[:, sl], True) * q_scale).astype(BF16)
            mk_ref[0, hd, rows, :] = (kv_up[:, sl] + k_pe).astype(BF16)
            mv_ref[0, hd, rows, :] = kv_up[:, A_WIDTH + hd * LANES:
                                           A_WIDTH + (hd + 1) * LANES].astype(BF16)

    def stage_chunk_proj(st):
        rows = st["rows"]
        cq_ref[0, rows, :] = (proj(st, 1280, 1536) * (CHK_DIM ** -0.5 * LOG2E)).astype(BF16)
        ck_ref[0, rows, :] = proj(st, 1536, 1792).astype(BF16)
        cv_ref[0, rows, :] = proj(st, 1792, 2048).astype(BF16)

    def stage_fox_store(st):
        rows = st["rows"]
        fq, fk, fv = st.pop("fq"), st.pop("fk"), st.pop("fv")
        for hd in range(FOX_HEADS):
            fq_ref[0, hd, rows, :] = jnp.where(lane < FOX_DIM, head_block(fq, hd) * f_scale,
                                               0.0).astype(BF16)
            fk_ref[0, hd, rows, :] = jnp.where(lane < FOX_DIM, head_block(fk, hd),
                                               0.0).astype(BF16)
            fv_ref[0, hd, rows, :] = jnp.where(lane < FOX_DIM, head_block(fv, hd),
                                               ones_col).astype(BF16)

    def stage_gate(st):
        z = st.pop("misc") + fb_ref[...]
        log_f = jnp.minimum(z, 0.0) - jnp.log1p(jnp.exp(-jnp.abs(z)))
        log_f = jnp.where(lane < FOX_HEADS, log_f, 0.0)
        p_hi = log_f.astype(BF16)
        rem = log_f - p_hi.astype(F32)
        p_mid = rem.astype(BF16)
        p_lo = (rem - p_mid.astype(F32)).astype(BF16)
        parts = jnp.dot(tri_ref[:th, :th], jnp.concatenate([p_hi, p_mid, p_lo], axis=-1),
                        preferred_element_type=F32)
        st["cum"] = parts[:, :LANES] + parts[:, LANES:2 * LANES] + parts[:, 2 * LANES:]

    for stage in (stage_norm, stage_latent, stage_up, stage_fox_proj, stage_mla_store,
                  stage_chunk_proj, stage_fox_store, stage_gate):
        for st in subs:
            stage(st)

    offset = carry_ref[...]
    for st in subs:
        cum = st["cum"] + offset
        offset = cum[th - 1:th, :]
        cum_ref[0, st["rows"], :] = cum * LOG2E
    carry_ref[...] = offset


def _in_proj(x, g, w, qn, wuq, kvn, wukv, tcos, ts1, ts2, fb, tri):
    b, s, _ = x.shape
    tm = TOK_TILE
    head_spec = pl.BlockSpec((1, 4, tm, LANES), lambda bi, i: (bi, 0, i, 0))
    dense_spec = pl.BlockSpec((1, tm, 256), lambda bi, i: (bi, i, 0))
    tab_spec = pl.BlockSpec((tm, LANES), lambda bi, i: (i, 0))
    head_shape = jax.ShapeDtypeStruct((b, 4, s, LANES), BF16)
    dense_shape = jax.ShapeDtypeStruct((b, s, 256), BF16)
    return pl.pallas_call(
        _in_proj_kernel,
        grid=(b, s // tm),
        in_specs=[pl.BlockSpec((1, tm, D_MODEL), lambda bi, i: (bi, i, 0)),
                  _const_spec((1, D_MODEL)), _const_spec((D_MODEL, IN_COLS)),
                  _const_spec((1, MLA_Q_RANK)), _const_spec((MLA_Q_RANK, 512)),
                  _const_spec((1, MLA_KV_RANK)), _const_spec((MLA_KV_RANK, 1024)),
                  tab_spec, tab_spec, tab_spec, _const_spec((1, LANES)), _const_spec((tm, tm))],
        out_specs=[head_spec] * 6 + [dense_spec] * 3
                  + [pl.BlockSpec((1, tm, LANES), lambda bi, i: (bi, i, 0))],
        out_shape=[head_shape] * 6 + [dense_shape] * 3
                  + [jax.ShapeDtypeStruct((b, s, LANES), F32)],
        scratch_shapes=[pltpu.VMEM((1, LANES), F32)],
        compiler_params=_params("arbitrary", "arbitrary"),
        name="in_proj",
    )(x, g, w, qn, wuq, kvn, wukv, tcos, ts1, ts2, fb, tri)


def _sweep_kernel(*refs, decay, chunk_mask, group):
    if decay:
        (q_ref, k_ref, v_ref, cq_ref, ck_ref, o_ref,
         m_sc, l_sc, acc_sc, s_sc, p_sc, alpha_sc, kmax_sc, ckmin_sc) = refs
    else:
        q_ref, k_ref, v_ref, o_ref, m_sc, l_sc, acc_sc, s_sc, p_sc, alpha_sc = refs
    i = pl.program_id(2)
    t = ATT_TILE
    heads = range(group)
    m_sc[...] = jnp.full_like(m_sc, NEG)
    l_sc[...] = jnp.zeros_like(l_sc)
    acc_sc[...] = jnp.zeros_like(acc_sc)
    if decay:
        nblk = ck_ref.shape[2]

        @pl.when(i == 0)
        def _():
            def key_norm(j, acc):
                kf = k_ref[0, 0, pl.ds(pl.multiple_of(j * t, t), t), :].astype(F32)
                return jnp.maximum(acc, jnp.max(jnp.sum(kf * kf, axis=-1, keepdims=True),
                                                axis=0, keepdims=True))
            kmax_sc[...] = jnp.sqrt(lax.fori_loop(0, nblk, key_norm, jnp.zeros((1, 1), F32)))
            ckmin_sc[...] = jnp.min(ck_ref[0, 0], axis=-1, keepdims=True)

        head_lane = _lane_iota((t, LANES)) == pl.program_id(1)
        cq = jnp.sum(jnp.where(head_lane, cq_ref[0], 0.0), axis=-1, keepdims=True)
        cq_rep = jnp.broadcast_to(cq, (t, LANES))

    def first_needed_block():
        if not decay:
            return 0
        qf = q_ref[0, 0].astype(F32)
        kd = k_ref[0, 0, pl.ds(pl.multiple_of(i * t, t), t), :].astype(F32)
        gap = (jnp.sqrt(jnp.sum(qf * qf, axis=-1, keepdims=True)) * kmax_sc[...]
               - jnp.sum(qf * kd, axis=-1, keepdims=True) + cq)
        gap_max = jnp.max(gap, axis=0, keepdims=True)
        needed = gap_max - ckmin_sc[...] >= -SKIP_LOG2
        blk = lax.broadcasted_iota(jnp.int32, (nblk, 1), 0)
        return jnp.min(jnp.where(needed, blk, i))

    def scores(j, slot):
        off = pl.multiple_of(j * t, t)
        for g in heads:
            k = k_ref[0, g, pl.ds(off, t), :]
            s_sc[g, slot] = lax.dot_general(q_ref[0, g], k, NT_DIMS, preferred_element_type=F32)

    def softmax(j, slot, masked, void=None):
        for g in heads:
            s = s_sc[g, slot]
            if decay:
                s = s - ck_ref[0, 0, pl.ds(j, 1), :]
            if void is not None:
                s = s + jnp.where(void, NEG, 0.0)
            if masked:
                qi = lax.broadcasted_iota(jnp.int32, (t, t), 0)
                ki = lax.broadcasted_iota(jnp.int32, (t, t), 1)
                if chunk_mask:
                    qi, ki = qi // CHUNK, ki // CHUNK
                s = jnp.where(ki <= qi, s, NEG)
            m_old = m_sc[g]
            m_cur = jnp.max(s, axis=-1, keepdims=True)
            if decay:
                m_cur = m_cur + cq_rep
            m_new = jnp.maximum(m_old, m_cur)
            alpha = jnp.exp2(m_old - m_new)
            shift = m_new - cq_rep if decay else m_new
            ps = [jnp.exp2(s[:, c * LANES:(c + 1) * LANES] - shift) for c in range(t // LANES)]
            if not decay:
                l_sc[g] = alpha * l_sc[g] + functools.reduce(lambda a, b: a + b, ps)
            p_sc[g, slot] = jnp.concatenate(ps, axis=-1).astype(BF16)
            alpha_sc[g, slot] = alpha
            m_sc[g] = m_new

    def weigh(j, slot):
        off = pl.multiple_of(j * t, t)
        for g in heads:
            v = v_ref[0, g, pl.ds(off, t), :]
            acc_sc[g] = alpha_sc[g, slot] * acc_sc[g] + jnp.dot(p_sc[g, slot], v,
                                                                preferred_element_type=F32)

    scores(i, 0)
    if decay:
        prev = jnp.maximum(i - 1, 0)
        scores(prev, 1)
        softmax(i, 0, True)
        first = first_needed_block()
        n_full = 1 + jnp.maximum(i - 1 - first, 0)
        block = lambda w: jnp.where(w == 1, prev, first + w - 2)
        void = lambda w: jnp.logical_and(i == 0, w == 1)
    else:
        scores(0, 1)
        softmax(i, 0, True)
        n_full = i
        block = lambda w: w - 1
        void = lambda w: None
    unroll = SWEEP_UNROLL
    trips = jnp.maximum(n_full - 1, 0) // unroll

    def body(tt, carry):
        w = 1 + unroll * tt
        weigh(jnp.where(tt == 0, i, block(w - 1)), 0)
        for u in range(unroll):
            scores(block(w + u + 1), u % 2)
            softmax(block(w + u), (u + 1) % 2, False, void(w + u) if u == 0 else None)
            if u + 1 < unroll:
                weigh(block(w + u), (u + 1) % 2)
        return carry

    lax.fori_loop(0, trips, body, 0)
    left = n_full - unroll * trips
    w_left = 1 + unroll * trips
    j_pending = jnp.where(trips == 0, i, block(w_left - 1))

    def tail(count):
        weigh(j_pending, 0)
        for u in range(count):
            if u + 1 < count:
                scores(block(w_left + u + 1), u % 2)
            softmax(block(w_left + u), (u + 1) % 2, False, void(w_left + u) if u == 0 else None)
            weigh(block(w_left + u), (u + 1) % 2)

    for count in range(unroll + 1):
        pl.when(left == count)(functools.partial(tail, count))

    for g in heads:
        acc = acc_sc[g]
        if decay:
            denom = acc[:, FOX_DIM:FOX_DIM + 1]
        else:
            denom = jnp.sum(l_sc[g], axis=-1, keepdims=True)
        o_ref[0, g] = acc / denom


def _sweep(q, k, v, cum=None, ck=None, *, chunk_mask):
    b, nh, s, _ = q.shape
    t = ATT_TILE
    decay = cum is not None
    group = 1 if decay else MLA_GROUP
    q_spec = pl.BlockSpec((1, group, t, LANES), lambda bi, hi, i: (bi, hi, i, 0))
    kv_spec = pl.BlockSpec((1, group, s, LANES), lambda bi, hi, i: (bi, hi, 0, 0))
    in_specs = [q_spec, kv_spec, kv_spec]
    args = [q, k, v]
    scratch = [pltpu.VMEM((group, t, LANES), F32), pltpu.VMEM((group, t, LANES), F32),
               pltpu.VMEM((group, t, LANES), F32), pltpu.VMEM((group, 2, t, t), F32),
               pltpu.VMEM((group, 2, t, t), BF16), pltpu.VMEM((group, 2, t, LANES), F32)]
    if decay:
        in_specs += [pl.BlockSpec((1, t, LANES), lambda bi, hi, i: (bi, i, 0)),
                     pl.BlockSpec((1, 1, s // t, t), lambda bi, hi, i: (bi, hi, 0, 0))]
        args += [cum, ck]
        scratch += [pltpu.VMEM((1, 1), F32), pltpu.VMEM((s // t, 1), F32)]
    return pl.pallas_call(
        functools.partial(_sweep_kernel, decay=decay, chunk_mask=chunk_mask, group=group),
        grid=(b, nh // group, s // t),
        in_specs=in_specs,
        out_specs=q_spec,
        out_shape=jax.ShapeDtypeStruct((b, nh, s, LANES), F32),
        scratch_shapes=scratch,
        compiler_params=_params("arbitrary", "arbitrary", "arbitrary"),
        name="fox_sweep" if decay else "mla_sweep",
    )(*args)


def _rel_bias_kernel(tab_ref, o_ref):
    qi = lax.broadcasted_iota(jnp.int32, (CHUNK, BAND), 0)
    ki = lax.broadcasted_iota(jnp.int32, (CHUNK, BAND), 1)
    idx = jnp.clip(qi + CHK_LEFT * CHUNK - ki, -(CHUNK - 1), REL_MAX) + (CHUNK - 1)
    for hd in range(CHK_HEADS):
        def body(r, acc):
            return jnp.where(idx == r, tab_ref[hd, r], acc)
        o_ref[hd] = lax.fori_loop(0, REL_SIZE, body, jnp.zeros((CHUNK, BAND), F32)) * LOG2E


def _rel_bias(table):
    return pl.pallas_call(
        _rel_bias_kernel,
        in_specs=[pl.BlockSpec(memory_space=pltpu.SMEM)],
        out_specs=pl.BlockSpec(memory_space=pltpu.VMEM),
        out_shape=jax.ShapeDtypeStruct((CHK_HEADS, CHUNK, BAND), F32),
        name="rel_bias",
    )(table)


def _chunk_kernel(q_ref, k_ref, v_ref, bias_ref, o_ref):
    lane = _lane_iota((CHK_TILE, C_WIDTH)) // CHK_DIM
    lead = CHK_LEFT * CHUNK // CHK_TILE
    tiles = CHK_STEP // CHK_TILE
    units = []
    for u in range(tiles):
        tile = pl.program_id(1) * tiles + u
        start = pl.multiple_of(jnp.maximum(tile - lead, 0) * CHK_TILE, CHK_TILE)
        shift = jnp.maximum(lead - tile, 0)
        units += [(u, hd, start, shift) for hd in range(CHK_HEADS)]

    def scores(unit):
        u, hd, start, _ = unit
        q = q_ref[0, u * CHK_TILE:(u + 1) * CHK_TILE, :]
        kw = k_ref[0, pl.ds(start, CHK_WIN), :]
        return lax.dot_general(jnp.where(lane == hd, q, jnp.zeros((), q.dtype)), kw, NT_DIMS,
                               preferred_element_type=F32)

    def attend(unit, s):
        _, hd, start, shift = unit
        s = s + bias_ref[shift, hd]
        m = jnp.max(s, axis=-1, keepdims=True)
        p = jnp.exp2(s - m)
        denom = jnp.sum(p, axis=-1, keepdims=True)
        vw = v_ref[0, pl.ds(start, CHK_WIN), :]
        full = jnp.dot(p.astype(BF16), vw, preferred_element_type=F32)
        return jnp.where(lane == hd, full / denom, 0.0)

    outs = [None] * tiles
    s_next = scores(units[0])
    for n, unit in enumerate(units):
        u, hd = unit[:2]
        s_cur = s_next
        if n + 1 < len(units):
            s_next = scores(units[n + 1])
        o = attend(unit, s_cur)
        outs[u] = o if outs[u] is None else outs[u] + o
        if hd == CHK_HEADS - 1:
            o_ref[0, u * CHK_TILE:(u + 1) * CHK_TILE, :] = outs[u]


def _chunk_attn(q, k, v, bias):
    b, s, _ = q.shape
    return pl.pallas_call(
        _chunk_kernel,
        grid=(b, s // CHK_STEP),
        in_specs=[pl.BlockSpec((1, CHK_STEP, C_WIDTH), lambda bi, i: (bi, i, 0)),
                  pl.BlockSpec((1, s, C_WIDTH), lambda bi, i: (bi, 0, 0)),
                  pl.BlockSpec((1, s, C_WIDTH), lambda bi, i: (bi, 0, 0)),
                  pl.BlockSpec(bias.shape, lambda bi, i: (0, 0, 0, 0),
                               pipeline_mode=pl.Buffered(1))],
        out_specs=pl.BlockSpec((1, CHK_STEP, C_WIDTH), lambda bi, i: (bi, i, 0)),
        out_shape=jax.ShapeDtypeStruct((b, s, C_WIDTH), F32),
        compiler_params=_params("arbitrary", "arbitrary"),
        name="chunk_attn",
    )(q, k, v, bias)


def _mem_kernel(m_ref, g_ref, w_ref, k_ref, v_ref):
    h = _rms(m_ref[0], g_ref[...], D_MODEL).astype(BF16)
    kv = jnp.dot(h, w_ref[...], preferred_element_type=F32)
    k_ref[0] = kv[:, :CROSS_WIDTH].astype(BF16)
    v_ref[0] = kv[:, CROSS_WIDTH:].astype(BF16)


def _mem_kv(mem, g, w, layer):
    b = mem.shape[0]
    spec = pl.BlockSpec((1, MEM_LEN, CROSS_WIDTH), lambda bi: (bi, 0, 0))
    shape = jax.ShapeDtypeStruct((b, MEM_LEN, CROSS_WIDTH), BF16)
    return pl.pallas_call(
        _mem_kernel,
        grid=(b,),
        in_specs=[pl.BlockSpec((1, MEM_LEN, D_MODEL), lambda bi: (bi, 0, 0)),
                  _const_spec((1, D_MODEL)), _layer_spec((D_MODEL, 2 * CROSS_WIDTH), layer)],
        out_specs=[spec, spec],
        out_shape=[shape, shape],
        compiler_params=_params("arbitrary"),
        name="mem_kv",
    )(mem, g, w)


def _mix_cross_kernel(x_ref, ya_ref, yb_ref, yc_ref, ga_ref, gb_ref, gc_ref, woa_ref, wob_ref,
                      woc_ref, gx_ref, wcq_ref, km_ref, vm_ref, wco_ref, o_ref):
    tm = x_ref.shape[1]
    th = tm // MIX_SPLIT
    lane = _lane_iota((th, LANES))
    heads = [slice(hd * CROSS_DIM, (hd + 1) * CROSS_DIM) for hd in range(CROSS_HEADS)]
    subs = [dict(rows=slice(r * th, (r + 1) * th)) for r in range(MIX_SPLIT)]

    def stage_mix(st):
        rows = st["rows"]
        ya = jnp.concatenate([ya_ref[0, hd, rows, :] for hd in range(MLA_HEADS)], axis=-1)
        yb = jnp.concatenate([jnp.where(lane < FOX_DIM, yb_ref[0, hd, rows, :], 0.0)
                              for hd in range(FOX_HEADS)], axis=-1)
        st["x1"] = (x_ref[0, rows, :]
                    + jnp.dot(_rms(ya, ga_ref[...], A_WIDTH).astype(BF16), woa_ref[...],
                              preferred_element_type=F32)
                    + jnp.dot(_rms(yb, gb_ref[...], B_WIDTH).astype(BF16), wob_ref[...],
                              preferred_element_type=F32)
                    + jnp.dot(_rms(yc_ref[0, rows, :], gc_ref[...], C_WIDTH).astype(BF16),
                              woc_ref[...], preferred_element_type=F32))

    def stage_query(st):
        h = _rms(st["x1"], gx_ref[...], D_MODEL).astype(BF16)
        st["q"] = (jnp.dot(h, wcq_ref[...], preferred_element_type=F32)
                   * (CROSS_DIM ** -0.5 * LOG2E)).astype(BF16)

    def stage_scores(st):
        q = st.pop("q")
        st["s"] = [lax.dot_general(q[:, sl], km_ref[0, :, sl], NT_DIMS,
                                   preferred_element_type=F32) for sl in heads]

    def stage_attend(st):
        outs = []
        for s, sl in zip(st.pop("s"), heads):
            m = jnp.max(s, axis=-1, keepdims=True)
            p = jnp.exp2(s - m)
            denom = jnp.sum(p, axis=-1, keepdims=True)
            outs.append(jnp.dot(p.astype(BF16), vm_ref[0, :, sl], preferred_element_type=F32)
                        / denom)
        st["o"] = jnp.concatenate(outs, axis=-1).astype(BF16)

    def stage_out(st):
        o_ref[0, st["rows"], :] = st.pop("x1") + jnp.dot(st.pop("o"), wco_ref[...],
                                                         preferred_element_type=F32)

    for stage in (stage_mix, stage_query, stage_scores, stage_attend, stage_out):
        for st in subs:
            stage(st)


def _mix_cross(x, ya, yb, yc, ga, gb, gc, wo, wob, gx, wcq, km, vm, wco, layer):
    b, s, _ = x.shape
    tm = TOK_TILE
    x_spec = pl.BlockSpec((1, tm, D_MODEL), lambda bi, i: (bi, i, 0))
    head_spec = pl.BlockSpec((1, 4, tm, LANES), lambda bi, i: (bi, 0, i, 0))
    mem_spec = pl.BlockSpec((1, MEM_LEN, CROSS_WIDTH), lambda bi, i: (bi, 0, 0))
    return pl.pallas_call(
        _mix_cross_kernel,
        grid=(b, s // tm),
        in_specs=[x_spec, head_spec, head_spec,
                  pl.BlockSpec((1, tm, C_WIDTH), lambda bi, i: (bi, i, 0)),
                  _const_spec((1, A_WIDTH)), _const_spec((1, 512)), _const_spec((1, C_WIDTH)),
                  _layer_spec((A_WIDTH, D_MODEL), layer), _const_spec((512, D_MODEL)),
                  _layer_spec((C_WIDTH, D_MODEL), layer, (A_WIDTH + B_WIDTH) // C_WIDTH),
                  _const_spec((1, D_MODEL)), _layer_spec((D_MODEL, CROSS_WIDTH), layer),
                  mem_spec, mem_spec, _layer_spec((CROSS_WIDTH, D_MODEL), layer)],
        out_specs=x_spec,
        out_shape=jax.ShapeDtypeStruct(x.shape, F32),
        compiler_params=_params("arbitrary", "arbitrary"),
        name="mix_cross",
    )(x, ya, yb, yc, ga, gb, gc, wo, wob, wo, gx, wcq, km, vm, wco)


def _ffn_kernel(x_ref, g_ref, wgu_ref, wd_ref, gf_ref, o_ref, acc_ref, *, final_norm):
    x = x_ref[0]
    h = _rms(x, g_ref[...], D_MODEL).astype(BF16)
    acc_ref[...] = x
    nblk = FFN_HIDDEN // FFN_BLOCK

    def gate_up(c):
        lo = c * FFN_BLOCK
        gate = jnp.dot(h, wgu_ref[:, lo:lo + FFN_BLOCK], preferred_element_type=F32)
        up = jnp.dot(h, wgu_ref[:, FFN_HIDDEN + lo:FFN_HIDDEN + lo + FFN_BLOCK],
                     preferred_element_type=F32)
        return gate, up

    nxt = gate_up(0)
    for c in range(nblk):
        gate, up = nxt
        if c + 1 < nblk:
            nxt = gate_up(c + 1)
        act = (gate * jax.nn.sigmoid(gate) * up).astype(BF16)
        acc_ref[...] += jnp.dot(act, wd_ref[c * FFN_BLOCK:(c + 1) * FFN_BLOCK, :],
                                preferred_element_type=F32)
    y = acc_ref[...]
    o_ref[0] = _rms(y, gf_ref[...], D_MODEL) if final_norm else y


def _ffn(x, g, wgu, wd, gf, layer, *, final_norm):
    b, s, _ = x.shape
    tm = TOK_TILE
    x_spec = pl.BlockSpec((1, tm, D_MODEL), lambda bi, i: (bi, i, 0))

    def resident(shape):
        return _layer_spec(shape, layer, pipeline_mode=pl.Buffered(1))

    return pl.pallas_call(
        functools.partial(_ffn_kernel, final_norm=final_norm),
        grid=(b, s // tm),
        in_specs=[x_spec, _const_spec((1, D_MODEL)),
                  resident((D_MODEL, 2 * FFN_HIDDEN)), resident((FFN_HIDDEN, D_MODEL)),
                  _const_spec((1, D_MODEL))],
        out_specs=x_spec,
        out_shape=jax.ShapeDtypeStruct(x.shape, F32),
        scratch_shapes=[pltpu.VMEM((tm, D_MODEL), F32)],
        compiler_params=_params("arbitrary", "arbitrary"),
        name="ffn",
    )(x, g, wgu, wd, gf)


def _pad_cols(a, width):
    return jnp.pad(a, ((0, 0), (0, width - a.shape[1])))


def _layer_weights(w_in, w_uq, w_ukv, w_o, out_norm, f_bias):
    sizes = (256, 128, 32, 256, 256, 256, 4, 256, 256, 256)
    parts, at = [], 0
    for n in sizes:
        parts.append(w_in[:, at:at + n])
        at += n
    c_q, c_kv, k_rope, fq, fk, fv, f_logit, cq, ck, cv = parts
    zeros = lambda n: jnp.zeros((D_MODEL, n), F32)
    misc = jnp.concatenate([f_logit, zeros(MLA_NOPE - FOX_HEADS), k_rope, zeros(32)], axis=1)
    w_in_r = jnp.concatenate([c_q, c_kv, misc, fq, fk, fv, cq, ck, cv], axis=1).astype(BF16)
    wuq = jnp.pad(w_uq.reshape(MLA_Q_RANK, MLA_HEADS, MLA_NOPE + MLA_ROPE),
                  ((0, 0), (0, 0), (0, 32))).reshape(MLA_Q_RANK, 512).astype(BF16)
    wukv = w_ukv.reshape(MLA_KV_RANK, MLA_HEADS, MLA_NOPE + MLA_V)
    wk = jnp.pad(wukv[:, :, :MLA_NOPE], ((0, 0), (0, 0), (0, 64))).reshape(MLA_KV_RANK, 512)
    wv = wukv[:, :, MLA_NOPE:].reshape(MLA_KV_RANK, 512)
    wukv_r = jnp.concatenate([wk, wv], axis=1).astype(BF16)
    wob = jnp.pad(w_o[A_WIDTH:A_WIDTH + B_WIDTH].reshape(FOX_HEADS, FOX_DIM, D_MODEL),
                  ((0, 0), (0, 64), (0, 0))).reshape(512, D_MODEL).astype(BF16)
    ga = out_norm[:A_WIDTH].reshape(1, A_WIDTH)
    gb = jnp.pad(out_norm[A_WIDTH:A_WIDTH + B_WIDTH].reshape(FOX_HEADS, FOX_DIM),
                 ((0, 0), (0, 64))).reshape(1, 512)
    gc = out_norm[A_WIDTH + B_WIDTH:].reshape(1, C_WIDTH)
    fb = _pad_cols(f_bias.reshape(1, FOX_HEADS), LANES)
    return w_in_r, wuq, wukv_r, wob, ga, gb, gc, fb


def _rope_tables(seq):
    pos = jnp.arange(seq, dtype=F32)
    inv = ROPE_THETA ** (-jnp.arange(0, MLA_ROPE, 2, dtype=F32) / MLA_ROPE)
    ang = pos[:, None] * inv[None, :]
    cos, sin = jnp.cos(ang), jnp.sin(ang)
    z = lambda n: jnp.zeros((seq, n), F32)
    half = MLA_ROPE // 2
    tcos = jnp.concatenate([z(MLA_NOPE), cos, cos, z(32)], axis=1)
    ts1 = jnp.concatenate([z(MLA_NOPE), -sin, z(half), z(32)], axis=1)
    ts2 = jnp.concatenate([z(MLA_NOPE), z(half), sin, z(32)], axis=1)
    return tcos, ts1, ts2


def _tile_bias(bias):
    nq = CHK_TILE // CHUNK
    rows = [jnp.pad(bias, ((0, 0), (0, 0), (c * CHUNK, CHK_WIN - BAND - c * CHUNK)),
                    constant_values=NEG) for c in range(nq)]
    tile = jnp.concatenate(rows, axis=1)
    lead = CHK_LEFT * CHUNK // CHK_TILE
    return jnp.stack([jnp.pad(tile[:, :, v * CHK_TILE:], ((0, 0), (0, 0), (0, v * CHK_TILE)),
                              constant_values=NEG) for v in range(lead + 1)])


def kernel(x, mem, norm_mix, w_in, q_norm, w_uq, kv_norm, w_ukv, f_bias, rel_bias, out_norm, w_o,
           norm_cross, norm_mem, w_cq, w_ckv, w_co, norm_ffn, w_gu, w_down, final_norm):
    b, s, _ = x.shape
    depth = w_in.shape[0]
    tcos, ts1, ts2 = _rope_tables(s)
    tri = jnp.tril(jnp.ones((TOK_TILE, TOK_TILE), BF16))
    row = lambda v: v.reshape(1, -1)
    wo_b, wcq_b, wckv_b, wco_b, wgu_b, wd_b = (w.astype(BF16)
                                                for w in (w_o, w_cq, w_ckv, w_co, w_gu, w_down))
    for l in range(depth):
        (w_in_r, wuq, wukv_r, wob, ga, gb, gc, fb) = _layer_weights(
            w_in[l], w_uq[l], w_ukv[l], w_o[l], out_norm[l], f_bias[l])
        mq, mk, mv, fq, fk, fv, cq, ck, cv, cum = _in_proj(
            x, row(norm_mix[l]), w_in_r, row(q_norm[l]), wuq, row(kv_norm[l]), wukv_r,
            tcos, ts1, ts2, fb, tri)
        ya = _sweep(mq, mk, mv, chunk_mask=True)
        cum_h = jnp.transpose(cum[:, :, :FOX_HEADS], (0, 2, 1))
        yb = _sweep(fq, fk, fv, cum, cum_h.reshape(b, FOX_HEADS, s // ATT_TILE, ATT_TILE),
                    chunk_mask=False)
        yc = _chunk_attn(cq, ck, cv, _tile_bias(_rel_bias(rel_bias[l])))
        km, vm = _mem_kv(mem, row(norm_mem[l]), wckv_b, l)
        x = _mix_cross(x, ya, yb, yc, ga, gb, gc, wo_b, wob, row(norm_cross[l]), wcq_b, km, vm,
                       wco_b, l)
        x = _ffn(x, row(norm_ffn[l]), wgu_b, wd_b, row(final_norm), l,
                 final_norm=(l == depth - 1))
    return x
```
